```python
import math, functools
import jax, jax.numpy as jnp
from jax import lax
import numpy as np

D_MODEL = 2048
BATCH = 2
SEQ = 4096
DEPTH = 4
DEC_BATCH = 8
DEC_SEQ = 1
PAST_LEN = 16384
PAGE_SIZE = 128

N_HEADS = 8
HEAD_DIM = 128
ATT_DIM = N_HEADS * HEAD_DIM
D_CONV = D_MODEL // 2
CONV_W = 3
N_IDX_HEADS = 16
D_IDX = 64
IDX_Q_DIM = N_IDX_HEADS * D_IDX
TOPK_MAX = 256
QBLK = 128
D_FF = ((8 * D_MODEL + 3 * 256 - 1) // (3 * 256)) * 256
ROPE_THETA = 10000.0
EPS = 1e-6
INDEX_SCALE = (D_IDX ** -0.5) * (N_IDX_HEADS ** -0.5)
SPLITS = (D_CONV, D_CONV, D_CONV, ATT_DIM, ATT_DIM, ATT_DIM, IDX_Q_DIM, D_IDX, N_IDX_HEADS, D_MODEL, D_MODEL)
PROJ_DIM = sum(SPLITS)
SPLIT_POINTS = tuple(int(s) for s in np.cumsum(SPLITS)[:-1])

kernel_name = "hybrid_conv_dsa_decoder_step"


def rmsnorm(x, g):
    x32 = x.astype(jnp.float32)
    y = x32 * lax.rsqrt(jnp.mean(x32 * x32, axis=-1, keepdims=True) + EPS)
    return y.astype(x.dtype) * g


def rope(x, pos):
    d = x.shape[-1]
    inv_freq = ROPE_THETA ** (-jnp.arange(0, d, 2, dtype=jnp.float32) / d)
    ang = pos.astype(jnp.float32)[:, None] * inv_freq[None, :]
    cos = jnp.cos(ang)[None, :, None, :]
    sin = jnp.sin(ang)[None, :, None, :]
    x32 = x.astype(jnp.float32)
    x1, x2 = x32[..., : d // 2], x32[..., d // 2:]
    out = jnp.concatenate([x1 * cos - x2 * sin, x2 * cos + x1 * sin], axis=-1)
    return out.astype(x.dtype)


def causal_conv(u, w, prev):
    T = u.shape[1]
    up = jnp.concatenate([prev, u], axis=1)
    y = sum(w[j] * up[:, j:j + T] for j in range(CONV_W))
    return y, up[:, -(CONV_W - 1):]


def index_scores(qi, ki, wi):
    dots = jnp.einsum('bthd,bsd->bths', qi, ki, preferred_element_type=jnp.float32)
    return jnp.einsum('bths,bth->bts', jax.nn.relu(dots), wi.astype(jnp.float32)) * INDEX_SCALE


def attend(q, ks, vs, valid):
    s = jnp.einsum('bthd,btkhd->bthk', q, ks, preferred_element_type=jnp.float32) * (HEAD_DIM ** -0.5)
    s = jnp.where(valid[:, :, None, :], s, -jnp.inf)
    p = jax.nn.softmax(s, axis=-1)
    return jnp.einsum('bthk,btkhd->bthd', p.astype(vs.dtype), vs)


def prompt_attend(q, k, v, qi, ki, wi):
    Bn, T = q.shape[:2]
    topk = min(TOPK_MAX, T // 4)
    nblk = T // QBLK
    bi = jnp.arange(Bn)[:, None, None]
    kpos = jnp.arange(T)

    def to_blocks(a):
        return jnp.moveaxis(a.reshape((Bn, nblk, QBLK) + a.shape[2:]), 1, 0)

    def blk(args):
        i, qb, qib, wib = args
        qpos = i * QBLK + jnp.arange(QBLK)
        s = index_scores(qib, ki, wib)
        s = jnp.where(kpos[None, None, :] <= qpos[None, :, None], s, -jnp.inf)
        _, idx = lax.top_k(s, topk)
        valid = idx <= qpos[None, :, None]
        return attend(qb, k[bi, idx], v[bi, idx], valid)

    out = lax.map(blk, (jnp.arange(nblk), to_blocks(q), to_blocks(qi), to_blocks(wi)))
    return jnp.moveaxis(out, 0, 1).reshape(Bn, T, N_HEADS, HEAD_DIM)


def sample_attend(q, k, v, qi, ki, wi, ck, cv, cki, page_table):
    Db, T = q.shape[:2]
    past = page_table.shape[1] * PAGE_SIZE
    L = past + T
    topk = min(TOPK_MAX, L // 4)
    bi = jnp.arange(Db)[:, None, None]
    ki_all = jnp.concatenate([cki[page_table].reshape(Db, past, D_IDX), ki], axis=1)
    s = index_scores(qi, ki_all, wi)
    qpos = past + jnp.arange(T)
    kpos = jnp.arange(L)
    s = jnp.where(kpos[None, None, :] <= qpos[None, :, None], s, -jnp.inf)
    _, idx = lax.top_k(s, topk)
    valid = idx <= qpos[None, :, None]
    in_past = (idx < past)[..., None, None]
    pidx = jnp.minimum(idx, past - 1)
    phys = page_table[bi, pidx // PAGE_SIZE]
    off = pidx % PAGE_SIZE
    nidx = jnp.clip(idx - past, 0, T - 1)
    ks = jnp.where(in_past, ck[phys, off], k[bi, nidx])
    vs = jnp.where(in_past, cv[phys, off], v[bi, nidx])
    return attend(q, ks, vs, valid)


def mixer(h, pos, w_in, conv_w, w_pa, w_pb, w_o, conv_prev, select_attend):
    Bn, T, _ = h.shape
    b_g, c_g, v_c, q, k, v, qi, ki, wi, g_a, g_b = jnp.split(h @ w_in, SPLIT_POINTS, axis=-1)
    conv_out, conv_state = causal_conv(c_g * v_c, conv_w, conv_prev)
    y_a = (b_g * conv_out) @ w_pa
    q = rope(q.reshape(Bn, T, N_HEADS, HEAD_DIM), pos)
    k = rope(k.reshape(Bn, T, N_HEADS, HEAD_DIM), pos)
    v = v.reshape(Bn, T, N_HEADS, HEAD_DIM)
    qi = rope(qi.reshape(Bn, T, N_IDX_HEADS, D_IDX), pos)
    ki = rope(ki[:, :, None, :], pos)[:, :, 0, :]
    att = select_attend(q, k, v, qi, ki, wi)
    y_b = att.reshape(Bn, T, ATT_DIM) @ w_pb
    merged = jax.nn.sigmoid(g_a) * y_a + jax.nn.sigmoid(g_b) * y_b
    return merged @ w_o, k, v, ki, conv_state


def layer(x, c, pos, w_mod, b_mod, g_mix, g_ffn, w_in, conv_w, w_pa, w_pb, w_o,
          w_gate, w_up, w_down, conv_prev, select_attend):
    mod = jax.nn.silu(c) @ w_mod + b_mod
    sh1, sc1, gt1, sh2, sc2, gt2 = jnp.split(mod[:, None, :], 6, axis=-1)
    h = rmsnorm(x, g_mix) * (1 + sc1) + sh1
    m, k, v, ki, cs = mixer(h, pos, w_in, conv_w, w_pa, w_pb, w_o, conv_prev, select_attend)
    x = x + gt1 * m
    h = rmsnorm(x, g_ffn) * (1 + sc2) + sh2
    x = x + gt2 * ((jax.nn.silu(h @ w_gate) * (h @ w_up)) @ w_down)
    return x, k, v, ki, cs


def setup_inputs(seed: int = 0) -> dict:
    key = jax.random.key(seed)
    ks = jax.random.split(key, 24)
    n_pages = PAST_LEN // PAGE_SIZE
    n_pool = (DEC_BATCH * n_pages * 5 + 3) // 4
    nrm = lambda k, shape, s=1.0: jax.random.normal(k, shape, jnp.float32) * s
    page_table = jax.random.permutation(ks[8], n_pool)[: DEC_BATCH * n_pages].reshape(DEC_BATCH, n_pages).astype(jnp.int32)
    return {
        "x_prompt": nrm(ks[0], (BATCH, SEQ, D_MODEL)),
        "x_sample": nrm(ks[1], (DEC_BATCH, DEC_SEQ, D_MODEL)),
        "c_prompt": nrm(ks[2], (BATCH, D_MODEL)),
        "c_sample": nrm(ks[3], (DEC_BATCH, D_MODEL)),
        "cache_k": nrm(ks[4], (DEPTH, n_pool, PAGE_SIZE, N_HEADS, HEAD_DIM)),
        "cache_v": nrm(ks[5], (DEPTH, n_pool, PAGE_SIZE, N_HEADS, HEAD_DIM)),
        "cache_kidx": nrm(ks[6], (DEPTH, n_pool, PAGE_SIZE, D_IDX)),
        "state_conv": nrm(ks[7], (DEPTH, DEC_BATCH, CONV_W - 1, D_CONV)),
        "page_table": page_table,
        "w_mod": nrm(ks[9], (DEPTH, D_MODEL, 6 * D_MODEL), 0.5 * D_MODEL ** -0.5),
        "b_mod": nrm(ks[10], (DEPTH, 6 * D_MODEL), 0.02),
        "g_mix": 1.0 + nrm(ks[11], (DEPTH, D_MODEL), 0.02),
        "g_ffn": 1.0 + nrm(ks[12], (DEPTH, D_MODEL), 0.02),
        "w_in": nrm(ks[13], (DEPTH, D_MODEL, PROJ_DIM), D_MODEL ** -0.5),
        "conv_w": nrm(ks[14], (DEPTH, CONV_W, D_CONV), CONV_W ** -0.5),
        "w_pa": nrm(ks[15], (DEPTH, D_CONV, D_MODEL), D_CONV ** -0.5),
        "w_pb": nrm(ks[16], (DEPTH, ATT_DIM, D_MODEL), ATT_DIM ** -0.5),
        "w_o": nrm(ks[17], (DEPTH, D_MODEL, D_MODEL), D_MODEL ** -0.5),
        "w_gate": nrm(ks[18], (DEPTH, D_MODEL, D_FF), D_MODEL ** -0.5),
        "w_up": nrm(ks[19], (DEPTH, D_MODEL, D_FF), D_MODEL ** -0.5),
        "w_down": nrm(ks[20], (DEPTH, D_FF, D_MODEL), D_FF ** -0.5),
        "g_final": 1.0 + nrm(ks[21], (D_MODEL,), 0.02),
    }


def reference(x_prompt, x_sample, c_prompt, c_sample, cache_k, cache_v, cache_kidx, state_conv,
              page_table, w_mod, b_mod, g_mix, g_ffn, w_in, conv_w, w_pa, w_pb, w_o,
              w_gate, w_up, w_down, g_final):
    Bp, Tp, _ = x_prompt.shape
    Db, Ts, _ = x_sample.shape
    pos_p = jnp.arange(Tp)
    pos_s = PAST_LEN + jnp.arange(Ts)
    xp, xs = x_prompt, x_sample
    kp_l, vp_l, kip_l, cp_l = [], [], [], []
    ks_l, vs_l, kis_l, cs_l = [], [], [], []
    for l in range(DEPTH):
        wl = (w_mod[l], b_mod[l], g_mix[l], g_ffn[l], w_in[l], conv_w[l], w_pa[l], w_pb[l], w_o[l],
              w_gate[l], w_up[l], w_down[l])
        conv0 = jnp.zeros((Bp, CONV_W - 1, D_CONV), xp.dtype)
        xp, kp, vp, kip, cp = layer(xp, c_prompt, pos_p, *wl, conv0, prompt_attend)
        samp = functools.partial(sample_attend, ck=cache_k[l], cv=cache_v[l], cki=cache_kidx[l],
                                 page_table=page_table)
        xs, kn, vn, kin, cn = layer(xs, c_sample, pos_s, *wl, state_conv[l], samp)
        kp_l.append(kp.reshape(Bp, Tp // PAGE_SIZE, PAGE_SIZE, N_HEADS, HEAD_DIM))
        vp_l.append(vp.reshape(Bp, Tp // PAGE_SIZE, PAGE_SIZE, N_HEADS, HEAD_DIM))
        kip_l.append(kip.reshape(Bp, Tp // PAGE_SIZE, PAGE_SIZE, D_IDX))
        cp_l.append(cp)
        ks_l.append(kn)
        vs_l.append(vn)
        kis_l.append(kin)
        cs_l.append(cn)
    y_prompt = rmsnorm(xp, g_final)
    y_sample = rmsnorm(xs, g_final)
    return (y_prompt, y_sample,
            jnp.stack(kp_l), jnp.stack(vp_l), jnp.stack(kip_l), jnp.stack(cp_l),
            jnp.stack(ks_l), jnp.stack(vs_l), jnp.stack(kis_l), jnp.stack(cs_l))
```

```python
import functools

import jax
import jax.numpy as jnp
import numpy as np
from jax import lax
from jax.experimental import pallas as pl
from jax.experimental.pallas import tpu as pltpu

F32 = jnp.float32
BF16 = jnp.bfloat16
I32 = jnp.int32

N_HEADS = 8
HEAD_DIM = 128
N_IDX_HEADS = 16
D_IDX = 64
CONV_W = 3
TOPK_MAX = 256
QBLK = 128
PAGE_SIZE = 128
ROPE_THETA = 10000.0
EPS = 1e-6
INDEX_SCALE = (D_IDX ** -0.5) * (N_IDX_HEADS ** -0.5)
ATT_SCALE = HEAD_DIM ** -0.5

LANES = 128
VMEM_LIMIT = 56 * 1024 * 1024
INT_MIN = -2 ** 31
M_FLOOR = -1e30
KEY_CHUNK = 256


def _params(sem):
    return pltpu.CompilerParams(dimension_semantics=sem, vmem_limit_bytes=VMEM_LIMIT)


def _dot(a, b):
    return jnp.dot(a, b, preferred_element_type=F32)


def _dot_nt(a, b):
    return lax.dot_general(a, b, (((1,), (1,)), ((), ())), preferred_element_type=F32)


def _rmsnorm(x):
    return x * lax.rsqrt(jnp.mean(x * x, axis=-1, keepdims=True) + EPS)


def _sort_key(s):
    s = jnp.where(s == 0.0, 0.0, s)
    b = pltpu.bitcast(s, I32)
    return jnp.where(b < 0, b ^ 0x7FFFFFFF, b)


def _mod_kernel(c_ref, w_ref, b_ref, o_ref):
    c = c_ref[...]
    a = (c * jax.nn.sigmoid(c)).astype(BF16)
    o_ref[...] = _dot(a, w_ref[...].astype(BF16)) + b_ref[...]


def _mod_call(c_all, w_mod, b_mod):
    depth, d, n = w_mod.shape
    rows = c_all.shape[0]
    tn = 1024
    return pl.pallas_call(
        _mod_kernel,
        grid=(depth, n // tn),
        in_specs=[pl.BlockSpec((rows, d), lambda l, j: (0, 0)),
                  pl.BlockSpec((None, d, tn), lambda l, j: (l, 0, j)),
                  pl.BlockSpec((None, 1, tn), lambda l, j: (l, 0, j))],
        out_specs=pl.BlockSpec((None, rows, tn), lambda l, j: (l, 0, j)),
        out_shape=jax.ShapeDtypeStruct((depth, rows, n), F32),
        compiler_params=_params(("arbitrary", "arbitrary")),
        name="mod",
    )(c_all, w_mod, b_mod.reshape(depth, 1, n))


def _mod_spec(mod, tm, row_axis):
    rm, d = mod.shape[1:]
    if rm == 1:
        return pl.BlockSpec((None, 1, d), lambda *ids: (ids[0], 0, 0))
    return pl.BlockSpec((None, tm, d), lambda *ids: (ids[0], ids[row_axis], 0))


def _norm_mod_kernel(x_ref, g_ref, sc_ref, sh_ref, o_ref):
    y = _rmsnorm(x_ref[...]) * g_ref[...]
    o_ref[...] = (y * (1.0 + sc_ref[...]) + sh_ref[...]).astype(o_ref.dtype)


def _norm_mod_call(x, g, sc, sh, tm):
    gn, r, d = x.shape
    return pl.pallas_call(
        _norm_mod_kernel,
        grid=(gn, r // tm),
        in_specs=[pl.BlockSpec((None, tm, d), lambda b, i: (b, i, 0)),
                  pl.BlockSpec((1, d), lambda b, i: (0, 0)),
                  _mod_spec(sc, tm, 1), _mod_spec(sh, tm, 1)],
        out_specs=pl.BlockSpec((None, tm, d), lambda b, i: (b, i, 0)),
        out_shape=jax.ShapeDtypeStruct((gn, r, d), BF16),
        compiler_params=_params(("arbitrary", "arbitrary")),
        name="norm_mod",
    )(x, g.reshape(1, d), sc, sh)


def _conv_seq_kernel(h_ref, w_ref, cw_ref, prev_ref, a_ref, st_ref, buf, *, tm):
    i = pl.program_id(2)
    h = h_ref[...]
    bg = _dot(h, w_ref[0])
    u = _dot(h, w_ref[1]) * _dot(h, w_ref[2])

    @pl.when(i == 0)
    def _():
        buf[6:8, :] = prev_ref[...]

    buf[8:8 + tm, :] = u
    cw = cw_ref[...]
    y = cw[0:1] * buf[6:6 + tm, :] + cw[1:2] * buf[7:7 + tm, :] + cw[2:3] * u
    a_ref[...] = (bg * y).astype(BF16)
    last2 = buf[6 + tm:8 + tm, :]
    buf[6:8, :] = last2
    st_ref[...] = last2


def _conv_seq_call(h, w3, cw, prev, tm, tn):
    gn, r, d = h.shape
    n = w3.shape[2]
    return pl.pallas_call(
        functools.partial(_conv_seq_kernel, tm=tm),
        grid=(gn, n // tn, r // tm),
        in_specs=[pl.BlockSpec((None, tm, d), lambda b, j, i: (b, i, 0)),
                  pl.BlockSpec((3, d, tn), lambda b, j, i: (0, 0, j)),
                  pl.BlockSpec((CONV_W, tn), lambda b, j, i: (0, j)),
                  pl.BlockSpec((None, CONV_W - 1, tn), lambda b, j, i: (b, 0, j))],
        out_specs=[pl.BlockSpec((None, tm, tn), lambda b, j, i: (b, i, j)),
                   pl.BlockSpec((None, CONV_W - 1, tn), lambda b, j, i: (b, 0, j))],
        out_shape=[jax.ShapeDtypeStruct((gn, r, n), BF16),
                   jax.ShapeDtypeStruct((gn, CONV_W - 1, n), F32)],
        scratch_shapes=[pltpu.VMEM((8 + tm, tn), F32)],
        compiler_params=_params(("arbitrary", "arbitrary", "arbitrary")),
        name="conv_seq",
    )(h, w3, cw, prev)


def _conv_step_kernel(h_ref, w_ref, cw_ref, p0_ref, p1_ref, a_ref, u_ref):
    h = h_ref[...]
    bg = _dot(h, w_ref[0])
    u = _dot(h, w_ref[1]) * _dot(h, w_ref[2])
    cw = cw_ref[...]
    y = cw[0:1] * p0_ref[...] + cw[1:2] * p1_ref[...] + cw[2:3] * u
    a_ref[...] = (bg * y).astype(BF16)
    u_ref[...] = u


def _conv_step_call(h, w3, cw, p0, p1, tn):
    r, d = h.shape
    n = w3.shape[2]
    row_tile = pl.BlockSpec((r, tn), lambda j: (0, j))
    return pl.pallas_call(
        _conv_step_kernel,
        grid=(n // tn,),
        in_specs=[pl.BlockSpec((r, d), lambda j: (0, 0)),
                  pl.BlockSpec((3, d, tn), lambda j: (0, 0, j)),
                  pl.BlockSpec((CONV_W, tn), lambda j: (0, j)),
                  row_tile, row_tile],
        out_specs=[row_tile, row_tile],
        out_shape=[jax.ShapeDtypeStruct((r, n), BF16), jax.ShapeDtypeStruct((r, n), F32)],
        compiler_params=_params(("arbitrary",)),
        name="conv_step",
    )(h, w3, cw, p0, p1)


def _rope_group(x, cos, sin, half):
    if 2 * half == LANES:
        rot = pltpu.roll(x, half, 1)
    else:
        lane = lax.broadcasted_iota(I32, x.shape, 1)
        rot = jnp.where(lane % (2 * half) < half, pltpu.roll(x, LANES - half, 1), pltpu.roll(x, half, 1))
    return x * cos + rot * sin


def _proj_kernel(h_ref, w_ref, cos_ref, sin_ref, *out_refs, half, emit_f32, emit_bf16, head_major):
    acc = _dot(h_ref[...], w_ref[...])
    cos, sin = cos_ref[...], sin_ref[...]
    groups = []
    for c in range(acc.shape[1] // LANES):
        x = acc[:, c * LANES:(c + 1) * LANES]
        groups.append(_rope_group(x, cos, sin, half) if half else x)
    outs = list(out_refs)
    if emit_f32:
        o = outs.pop(0)
        for c, x in enumerate(groups):
            o[:, c * LANES:(c + 1) * LANES] = x
    if emit_bf16:
        o = outs.pop(0)
        for c, x in enumerate(groups):
            o[:, c * LANES:(c + 1) * LANES] = x.astype(BF16)
    if head_major:
        o = outs.pop(0)
        for c, x in enumerate(groups):
            xb = x.astype(BF16)
            o[2 * c] = xb[:, :D_IDX]
            o[2 * c + 1] = xb[:, D_IDX:]


def _proj_call(h, w, cos, sin, tm, *, half, emit_f32=False, emit_bf16=False, head_major=False):
    gn, r, d = h.shape
    n = w.shape[1]
    row_out = pl.BlockSpec((None, tm, n), lambda b, i: (b, i, 0))
    out_specs, out_shape = [], []
    if emit_f32:
        out_specs.append(row_out)
        out_shape.append(jax.ShapeDtypeStruct((gn, r, n), F32))
    if emit_bf16:
        out_specs.append(row_out)
        out_shape.append(jax.ShapeDtypeStruct((gn, r, n), BF16))
    if head_major:
        out_specs.append(pl.BlockSpec((None, n // D_IDX, tm, D_IDX), lambda b, i: (b, 0, i, 0)))
        out_shape.append(jax.ShapeDtypeStruct((gn, n // D_IDX, r, D_IDX), BF16))
    return pl.pallas_call(
        functools.partial(_proj_kernel, half=half, emit_f32=emit_f32, emit_bf16=emit_bf16, head_major=head_major),
        grid=(gn, r // tm),
        in_specs=[pl.BlockSpec((None, tm, d), lambda b, i: (b, i, 0)),
                  pl.BlockSpec((d, n), lambda b, i: (0, 0)),
                  pl.BlockSpec((tm, LANES), lambda b, i: (i, 0)),
                  pl.BlockSpec((tm, LANES), lambda b, i: (i, 0))],
        out_specs=out_specs, out_shape=out_shape,
        compiler_params=_params(("arbitrary", "arbitrary")),
        name="proj",
    )(h, w, cos, sin)


def _kidx_kernel(h_ref, w_ref, cos_ref, sin_ref, kw_ref, kb_ref):
    acc = _dot(h_ref[...], w_ref[...])
    roped = _rope_group(acc, cos_ref[...], sin_ref[...], D_IDX // 2)
    lane = lax.broadcasted_iota(I32, acc.shape, 1)
    kw = jnp.where(lane < D_IDX, roped, acc)
    kw_ref[...] = kw
    kb_ref[...] = kw[:, :D_IDX].astype(BF16)


def _kidx_call(h, w, cos, sin, tm):
    gn, r, d = h.shape
    return pl.pallas_call(
        _kidx_kernel,
        grid=(gn, r // tm),
        in_specs=[pl.BlockSpec((None, tm, d), lambda b, i: (b, i, 0)),
                  pl.BlockSpec((d, LANES), lambda b, i: (0, 0)),
                  pl.BlockSpec((tm, LANES), lambda b, i: (i, 0)),
                  pl.BlockSpec((tm, LANES), lambda b, i: (i, 0))],
        out_specs=[pl.BlockSpec((None, tm, LANES), lambda b, i: (b, i, 0)),
                   pl.BlockSpec((None, tm, D_IDX), lambda b, i: (b, i, 0))],
        out_shape=[jax.ShapeDtypeStruct((gn, r, LANES), F32), jax.ShapeDtypeStruct((gn, r, D_IDX), BF16)],
        compiler_params=_params(("arbitrary", "arbitrary")),
        name="kidx",
    )(h, w, cos, sin)


def _strict_upper(n, dtype):
    return jnp.where(lax.broadcasted_iota(I32, (n, n), 0) < lax.broadcasted_iota(I32, (n, n), 1), 1.0, 0.0).astype(dtype)


def _dsa_kernel(q_ref, qi_ref, kw_ref, k_ref, v_ref, ki_ref, o_ref,
                key_scr, wb_scr, m_scr, l_scr, acc_scr, *, topk, ck):
    i = pl.program_id(1)
    nch = (i * QBLK + QBLK + ck - 1) // ck
    ngrp = ck // LANES
    qpos = i * QBLK + lax.broadcasted_iota(I32, (QBLK, LANES), 0)
    lane = lax.broadcasted_iota(I32, (QBLK, LANES), 1)

    kw = kw_ref[...]
    for h in range(N_IDX_HEADS):
        wb_scr[h] = jnp.broadcast_to(kw[:, D_IDX + h:D_IDX + h + 1], (QBLK, LANES))

    def index_chunk(c, _):
        base = pl.multiple_of(c * ck, ck)
        kc = ki_ref[pl.ds(base, ck), :]
        acc = [jnp.zeros((QBLK, LANES), F32) for _ in range(ngrp)]
        for h in range(N_IDX_HEADS):
            d = _dot_nt(qi_ref[h], kc)
            wbh = wb_scr[h]
            for g in range(ngrp):
                acc[g] = acc[g] + jnp.maximum(d[:, g * LANES:(g + 1) * LANES], 0.0) * wbh
        for g in range(ngrp):
            key = _sort_key(acc[g] * INDEX_SCALE)
            kpos = base + g * LANES + lane
            key_scr[c, :, g * LANES:(g + 1) * LANES] = jnp.where(kpos <= qpos, key, INT_MIN)
        return 0

    lax.fori_loop(0, nch, index_chunk, 0)

    def count(pred, thr):
        thr = jnp.broadcast_to(thr, (QBLK, LANES))

        def body(c, cnt):
            kc = key_scr[c]
            for g in range(ngrp):
                cnt = cnt + jnp.where(pred(kc[:, g * LANES:(g + 1) * LANES], thr), 1, 0)
            return cnt
        cnt = lax.fori_loop(0, nch, body, jnp.zeros((QBLK, LANES), I32))
        return jnp.sum(cnt, axis=1, keepdims=True)

    ge = lambda a, b: a >= b
    gt = lambda a, b: a > b

    def search(it, t):
        cand = t + jnp.left_shift(jnp.int32(1), 31 - it)
        return jnp.where(count(ge, cand) >= topk, cand, t)

    t = lax.fori_loop(0, 32, search, jnp.full((QBLK, 1), INT_MIN, I32))
    t = jnp.maximum(t, INT_MIN + 1)
    c_ge = count(ge, t)

    @pl.when(jnp.max(c_ge) > topk)
    def _():
        need = (topk - count(gt, t)).astype(F32)
        upper = _strict_upper(ck, BF16)

        def body(c, carry):
            kc = key_scr[c]
            eq = kc == t
            eqf = jnp.where(eq, 1.0, 0.0)
            before = _dot(eqf.astype(BF16), upper) + carry
            key_scr[c] = jnp.where(eq & (before >= need), INT_MIN, kc)
            return carry + jnp.sum(eqf, axis=1, keepdims=True)

        lax.fori_loop(0, nch, body, jnp.zeros((QBLK, 1), F32))

    m_scr[...] = jnp.full(m_scr.shape, M_FLOOR, F32)
    l_scr[...] = jnp.zeros(l_scr.shape, F32)
    acc_scr[...] = jnp.zeros(acc_scr.shape, F32)

    def attend_chunk(c, _):
        base = pl.multiple_of(c * ck, ck)
        sel = key_scr[c] >= t
        for h in range(N_HEADS):
            cols = slice(h * HEAD_DIM, (h + 1) * HEAD_DIM)
            s = _dot_nt(q_ref[:, cols], k_ref[pl.ds(base, ck), cols]) * ATT_SCALE
            s = jnp.where(sel, s, -jnp.inf)
            m_old = m_scr[h]
            m_new = jnp.maximum(m_old, jnp.max(s, axis=1, keepdims=True))
            alpha = jnp.exp(m_old - m_new)
            p = jnp.exp(s - m_new)
            l_scr[h] = alpha * l_scr[h] + jnp.sum(p, axis=1, keepdims=True)
            acc_scr[h] = alpha * acc_scr[h] + _dot(p.astype(BF16), v_ref[pl.ds(base, ck), cols])
            m_scr[h] = m_new
        return 0

    lax.fori_loop(0, nch, attend_chunk, 0)
    for h in range(N_HEADS):
        o_ref[:, h * HEAD_DIM:(h + 1) * HEAD_DIM] = (acc_scr[h] / l_scr[h]).astype(o_ref.dtype)


def _dsa_call(q, qi_hm, kw, k, v, ki, topk):
    bn, t, a = q.shape
    ck = min(KEY_CHUNK, t)
    full = lambda shape: pl.BlockSpec((None,) + shape, lambda b, i: (b, 0, 0))
    return pl.pallas_call(
        functools.partial(_dsa_kernel, topk=topk, ck=ck),
        grid=(bn, t // QBLK),
        in_specs=[pl.BlockSpec((None, QBLK, a), lambda b, i: (b, i, 0)),
                  pl.BlockSpec((None, N_IDX_HEADS, QBLK, D_IDX), lambda b, i: (b, 0, i, 0)),
                  pl.BlockSpec((None, QBLK, LANES), lambda b, i: (b, i, 0)),
                  full((t, a)), full((t, a)), full((t, D_IDX))],
        out_specs=pl.BlockSpec((None, QBLK, a), lambda b, i: (b, i, 0)),
        out_shape=jax.ShapeDtypeStruct((bn, t, a), BF16),
        scratch_shapes=[pltpu.VMEM((t // ck, QBLK, ck), I32),
                        pltpu.VMEM((N_IDX_HEADS, QBLK, LANES), F32),
                        pltpu.VMEM((N_HEADS, QBLK, 1), F32),
                        pltpu.VMEM((N_HEADS, QBLK, 1), F32),
                        pltpu.VMEM((N_HEADS, QBLK, HEAD_DIM), F32)],
        compiler_params=_params(("arbitrary", "arbitrary")),
        name="dsa_prompt",
    )(q, qi_hm, kw, k, v, ki)


def _pool_scores_kernel(qi_ref, w_ref, ck_ref, o_ref, *, nb):
    kc = ck_ref[...].astype(BF16)
    d = jnp.maximum(_dot_nt(qi_ref[...], kc), 0.0) * w_ref[...]
    s = d[0:nb]
    for h in range(1, N_IDX_HEADS):
        s = s + d[h * nb:(h + 1) * nb]
    o_ref[...] = s * INDEX_SCALE


def _pool_scores_call(qi_hb, w_hb, cache_kidx, layer, nb, pages_per_step):
    depth, n_pool, page, di = cache_kidx.shape
    flat = cache_kidx.reshape(depth, n_pool * page, di)
    tk = pages_per_step * page
    return pl.pallas_call(
        functools.partial(_pool_scores_kernel, nb=nb),
        grid=(n_pool // pages_per_step,),
        in_specs=[pl.BlockSpec(qi_hb.shape, lambda j: (0, 0)),
                  pl.BlockSpec(w_hb.shape, lambda j: (0, 0)),
                  pl.BlockSpec((None, tk, di), lambda j: (layer, j, 0))],
        out_specs=pl.BlockSpec((nb, tk), lambda j: (0, j)),
        out_shape=jax.ShapeDtypeStruct((nb, n_pool * page), F32),
        compiler_params=_params(("arbitrary",)),
        name="pool_scores",
    )(qi_hb, w_hb, flat)


def _select_kernel(pt_ref, sc_ref, qi_ref, w_ref, kn_ref, idx_ref, s_scr, rk_scr, *, topk, n_pages, nb):
    b = pl.program_id(0)
    past = n_pages * PAGE_SIZE

    def gather_page(p, _):
        s_scr[pl.ds(p, 1), :] = sc_ref[pl.ds(pt_ref[b, p], 1), :]
        return 0

    lax.fori_loop(0, n_pages, gather_page, 0)
    key = _sort_key(s_scr[...])

    d = jnp.maximum(_dot_nt(qi_ref[...], kn_ref[...]), 0.0) * w_ref[...]
    row = lax.broadcasted_iota(I32, d.shape, 0)
    col = lax.broadcasted_iota(I32, d.shape, 1)
    mine = jnp.where((row % nb == b) & (col == b), d, 0.0)
    s_new = jnp.sum(jnp.sum(mine, axis=1, keepdims=True), axis=0, keepdims=True) * INDEX_SCALE
    key_new = _sort_key(s_new)

    def total(x):
        return jnp.sum(jnp.sum(x, axis=1, keepdims=True), axis=0, keepdims=True)

    def count_ge(thr):
        return total(jnp.where(key >= thr, 1, 0)) + jnp.where(key_new >= thr, 1, 0)

    def search(it, t):
        cand = t + jnp.left_shift(jnp.int32(1), 31 - it)
        return jnp.where(count_ge(cand) >= topk, cand, t)

    t = lax.fori_loop(0, 32, search, jnp.full((1, 1), INT_MIN, I32))
    t = jnp.maximum(t, INT_MIN + 1)

    upper = _strict_upper(PAGE_SIZE, BF16)
    lower = jnp.where(lax.broadcasted_iota(I32, (n_pages, n_pages), 1) < lax.broadcasted_iota(I32, (n_pages, n_pages), 0),
                      1.0, 0.0).astype(BF16)

    def ranks(flags):
        f = jnp.where(flags, 1.0, 0.0)
        rows = jnp.broadcast_to(jnp.sum(f, axis=1, keepdims=True), f.shape)
        return _dot(f.astype(BF16), upper) + _dot(lower, rows.astype(BF16)), total(f)

    n_gt = total(jnp.where(key > t, 1.0, 0.0)) + jnp.where(key_new > t, 1.0, 0.0)
    need = topk - n_gt
    eq = key == t
    eq_before, n_eq = ranks(eq)
    sel = (key > t) | (eq & (eq_before < need))
    sel_new = (key_new > t) | ((key_new == t) & (n_eq < need))

    rank, n_sel = ranks(sel)
    rk_scr[...] = jnp.where(sel, rank, -1.0)
    slot = lax.broadcasted_iota(I32, (topk, PAGE_SIZE), 0).astype(F32)
    off = lax.broadcasted_iota(I32, (topk, PAGE_SIZE), 1)

    def place(p, acc):
        hit = slot == jnp.broadcast_to(rk_scr[pl.ds(p, 1), :], slot.shape)
        return acc + jnp.where(hit, p * PAGE_SIZE + off, 0)

    acc = lax.fori_loop(0, n_pages, place, jnp.zeros((topk, PAGE_SIZE), I32))
    idx = jnp.sum(acc, axis=1, keepdims=True)
    idx_ref[...] = idx + jnp.where(sel_new & (slot[:, 0:1] == n_sel), past, 0)


def _select_call(page_table, scores, qi_hb, w_hb, k_new, topk):
    nb, n_pages = page_table.shape
    n_pool = scores.shape[1]
    return pl.pallas_call(
        functools.partial(_select_kernel, topk=topk, n_pages=n_pages, nb=nb),
        grid_spec=pltpu.PrefetchScalarGridSpec(
            num_scalar_prefetch=1, grid=(nb,),
            in_specs=[pl.BlockSpec((None, n_pool, PAGE_SIZE), lambda b, pt: (b, 0, 0)),
                      pl.BlockSpec(qi_hb.shape, lambda b, pt: (0, 0)),
                      pl.BlockSpec(w_hb.shape, lambda b, pt: (0, 0)),
                      pl.BlockSpec(k_new.shape, lambda b, pt: (0, 0))],
            out_specs=pl.BlockSpec((None, topk, 1), lambda b, pt: (b, 0, 0)),
            scratch_shapes=[pltpu.VMEM((n_pages, PAGE_SIZE), F32), pltpu.VMEM((n_pages, PAGE_SIZE), F32)]),
        out_shape=jax.ShapeDtypeStruct((nb, topk, 1), I32),
        compiler_params=_params(("arbitrary",)),
        name="select",
    )(page_table, scores, qi_hb, w_hb, k_new)


def _gather_attend_kernel(idx_ref, pt_ref, q_ref, kn_hbm, vn_hbm, ck_hbm, cv_hbm, o_ref,
                          kbuf, vbuf, sem, *, layer, topk, n_pages):
    b = pl.program_id(0)
    past = n_pages * PAGE_SIZE

    def new_copies(r):
        return (pltpu.make_async_copy(kn_hbm.at[b], kbuf.at[r], sem.at[0]),
                pltpu.make_async_copy(vn_hbm.at[b], vbuf.at[r], sem.at[1]))

    def start(r, _):
        pos = idx_ref[b, r]

        @pl.when(pos < past)
        def _():
            phys = pt_ref[b, pos // PAGE_SIZE]
            off = pos % PAGE_SIZE
            pltpu.make_async_copy(ck_hbm.at[layer, phys, off], kbuf.at[r], sem.at[0]).start()
            pltpu.make_async_copy(cv_hbm.at[layer, phys, off], vbuf.at[r], sem.at[1]).start()

        @pl.when(pos >= past)
        def _():
            for cp in new_copies(r):
                cp.start()
        return 0

    def wait(r, _):
        for cp in new_copies(r):
            cp.wait()
        return 0

    lax.fori_loop(0, topk, start, 0)
    lax.fori_loop(0, topk, wait, 0)

    q = q_ref[...]
    s = jnp.sum(kbuf[...] * q[None], axis=-1, keepdims=True) * ATT_SCALE
    p = jnp.exp(s - jnp.max(s, axis=0, keepdims=True))
    p = p / jnp.sum(p, axis=0, keepdims=True)
    o_ref[...] = jnp.sum(p * vbuf[...], axis=0)


def _gather_attend_call(idx, page_table, q, k_new, v_new, cache_k, cache_v, layer, topk):
    nb, n_pages = page_table.shape
    any_spec = pl.BlockSpec(memory_space=pl.ANY)
    return pl.pallas_call(
        functools.partial(_gather_attend_kernel, layer=layer, topk=topk, n_pages=n_pages),
        grid_spec=pltpu.PrefetchScalarGridSpec(
            num_scalar_prefetch=2, grid=(nb,),
            in_specs=[pl.BlockSpec((None, N_HEADS, HEAD_DIM), lambda b, idx, pt: (b, 0, 0)),
                      any_spec, any_spec, any_spec, any_spec],
            out_specs=pl.BlockSpec((None, N_HEADS, HEAD_DIM), lambda b, idx, pt: (b, 0, 0)),
            scratch_shapes=[pltpu.VMEM((topk, N_HEADS, HEAD_DIM), F32),
                            pltpu.VMEM((topk, N_HEADS, HEAD_DIM), F32),
                            pltpu.SemaphoreType.DMA((2,))]),
        out_shape=jax.ShapeDtypeStruct((nb, N_HEADS, HEAD_DIM), F32),
        compiler_params=_params(("arbitrary",)),
        name="gather_attend",
    )(idx, page_table, q, k_new, v_new, cache_k, cache_v)


def _merge_kernel(h_ref, a_ref, t_ref, wga_ref, wgb_ref, wpa_ref, wpb_ref, o_ref):
    h = h_ref[...]
    ya = _dot(a_ref[...], wpa_ref[...])
    yb = _dot(t_ref[...], wpb_ref[...])
    ga = jax.nn.sigmoid(_dot(h, wga_ref[...]))
    gb = jax.nn.sigmoid(_dot(h, wgb_ref[...]))
    o_ref[...] = (ga * ya + gb * yb).astype(BF16)


def _merge_call(h, a_in, att, wga, wgb, wpa, wpb, tm, tn):
    gn, r, d = h.shape
    a = a_in.shape[2]
    rows = lambda w: pl.BlockSpec((None, tm, w), lambda b, j, i: (b, i, 0))
    cols = lambda k: pl.BlockSpec((k, tn), lambda b, j, i: (0, j))
    return pl.pallas_call(
        _merge_kernel,
        grid=(gn, d // tn, r // tm),
        in_specs=[rows(d), rows(a), rows(a), cols(d), cols(d), cols(a), cols(a)],
        out_specs=pl.BlockSpec((None, tm, tn), lambda b, j, i: (b, i, j)),
        out_shape=jax.ShapeDtypeStruct((gn, r, d), BF16),
        compiler_params=_params(("arbitrary", "arbitrary", "arbitrary")),
        name="merge",
    )(h, a_in, att, wga, wgb, wpa, wpb)


def _resid_norm_kernel(x_ref, m_ref, w_ref, gt_ref, g_ref, sc_ref, sh_ref, x_out, h_out):
    x = x_ref[...] + gt_ref[...] * _dot(m_ref[...], w_ref[...])
    x_out[...] = x
    y = _rmsnorm(x) * g_ref[...]
    h_out[...] = (y * (1.0 + sc_ref[...]) + sh_ref[...]).astype(BF16)


def _resid_norm_call(x, m, w, gt, g, sc, sh, tm):
    gn, r, d = x.shape
    row = lambda: pl.BlockSpec((None, tm, d), lambda b, i: (b, i, 0))
    return pl.pallas_call(
        _resid_norm_kernel,
        grid=(gn, r // tm),
        in_specs=[row(), row(), pl.BlockSpec((d, d), lambda b, i: (0, 0)),
                  _mod_spec(gt, tm, 1), pl.BlockSpec((1, d), lambda b, i: (0, 0)),
                  _mod_spec(sc, tm, 1), _mod_spec(sh, tm, 1)],
        out_specs=[row(), row()],
        out_shape=[jax.ShapeDtypeStruct((gn, r, d), F32), jax.ShapeDtypeStruct((gn, r, d), BF16)],
        compiler_params=_params(("arbitrary", "arbitrary")),
        name="resid_norm",
    )(x, m, w, gt, g.reshape(1, d), sc, sh)


def _ffn_up_kernel(h_ref, wg_ref, wu_ref, o_ref):
    h = h_ref[...]
    g = _dot(h, wg_ref[...])
    o_ref[...] = (g * jax.nn.sigmoid(g) * _dot(h, wu_ref[...])).astype(BF16)


def _ffn_up_call(h, wg, wu, tm, tn):
    gn, r, d = h.shape
    f = wg.shape[1]
    return pl.pallas_call(
        _ffn_up_kernel,
        grid=(gn, f // tn, r // tm),
        in_specs=[pl.BlockSpec((None, tm, d), lambda b, j, i: (b, i, 0)),
                  pl.BlockSpec((d, tn), lambda b, j, i: (0, j)),
                  pl.BlockSpec((d, tn), lambda b, j, i: (0, j))],
        out_specs=pl.BlockSpec((None, tm, tn), lambda b, j, i: (b, i, j)),
        out_shape=jax.ShapeDtypeStruct((gn, r, f), BF16),
        compiler_params=_params(("arbitrary", "arbitrary", "arbitrary")),
        name="ffn_up",
    )(h, wg, wu)


def _ffn_down_kernel(x_ref, a_ref, w_ref, gt_ref, g_ref, sc_ref, sh_ref, x_out, n_out, acc, *, final):
    k = pl.program_id(2)

    @pl.when(k == 0)
    def _():
        acc[...] = jnp.zeros(acc.shape, F32)

    acc[...] += _dot(a_ref[...], w_ref[...])

    @pl.when(k == pl.num_programs(2) - 1)
    def _():
        x = x_ref[...] + gt_ref[...] * acc[...]
        x_out[...] = x
        y = _rmsnorm(x) * g_ref[...]
        if not final:
            y = y * (1.0 + sc_ref[...]) + sh_ref[...]
        n_out[...] = y.astype(n_out.dtype)


def _ffn_down_call(x, act, w, gt, g, sc, sh, tm, tk, final):
    gn, r, d = x.shape
    f = act.shape[2]
    row = lambda: pl.BlockSpec((None, tm, d), lambda b, i, k: (b, i, 0))
    return pl.pallas_call(
        functools.partial(_ffn_down_kernel, final=final),
        grid=(gn, r // tm, f // tk),
        in_specs=[row(), pl.BlockSpec((None, tm, tk), lambda b, i, k: (b, i, k)),
                  pl.BlockSpec((tk, d), lambda b, i, k: (k, 0)),
                  _mod_spec(gt, tm, 1), pl.BlockSpec((1, d), lambda b, i, k: (0, 0)),
                  _mod_spec(sc, tm, 1), _mod_spec(sh, tm, 1)],
        out_specs=[row(), row()],
        out_shape=[jax.ShapeDtypeStruct((gn, r, d), F32),
                   jax.ShapeDtypeStruct((gn, r, d), F32 if final else BF16)],
        scratch_shapes=[pltpu.VMEM((tm, d), F32)],
        compiler_params=_params(("arbitrary", "arbitrary", "arbitrary")),
        name="ffn_down",
    )(x, act, w, gt, g.reshape(1, d), sc, sh)


def _rope_tables(pos, d):
    inv_freq = ROPE_THETA ** (-jnp.arange(0, d, 2, dtype=F32) / d)
    ang = pos.astype(F32)[:, None] * inv_freq[None, :]
    cos, sin = jnp.cos(ang), jnp.sin(ang)
    reps = LANES // d
    return (jnp.tile(jnp.concatenate([cos, cos], axis=1), (1, reps)),
            jnp.tile(jnp.concatenate([-sin, sin], axis=1), (1, reps)))


def _pages_per_step(n_pool):
    return max(p for p in range(1, 33) if n_pool % p == 0)


def _row_tile(r, want):
    return min(r, want)


def kernel(x_prompt, x_sample, c_prompt, c_sample, cache_k, cache_v, cache_kidx, state_conv, page_table,
           w_mod, b_mod, g_mix, g_ffn, w_in, conv_w, w_pa, w_pb, w_o, w_gate, w_up, w_down, g_final):
    bp, tp, d = x_prompt.shape
    db, ts, _ = x_sample.shape
    assert ts == 1, "the sample path handles one new token per sequence"
    depth = w_mod.shape[0]
    n_pages = page_table.shape[1]
    past = n_pages * PAGE_SIZE
    att_dim = N_HEADS * HEAD_DIM
    d_conv = d // 2
    idx_q = N_IDX_HEADS * D_IDX
    topk_p = min(TOPK_MAX, tp // 4)
    topk_s = min(TOPK_MAX, (past + ts) // 4)
    rs = 16
    assert db <= rs and tp % QBLK == 0

    o_bg, o_cg, o_vc = 0, d_conv, 2 * d_conv
    o_q = 3 * d_conv
    o_k, o_v, o_qi = o_q + att_dim, o_q + 2 * att_dim, o_q + 3 * att_dim
    o_ki = o_qi + idx_q
    o_wi = o_ki + D_IDX
    o_ga = o_wi + N_IDX_HEADS
    o_gb = o_ga + d

    c_all = jnp.zeros((rs, d), F32).at[:bp].set(c_prompt).at[bp:bp + db].set(c_sample)
    mod = _mod_call(c_all, w_mod, b_mod)
    mod_p = mod[:, :bp].reshape(depth, bp, 1, 6, d)
    mod_s = jnp.zeros((depth, 1, rs, 6, d), F32).at[:, 0, :db].set(mod[:, bp:bp + db].reshape(depth, db, 6, d))

    cos_p, sin_p = _rope_tables(jnp.arange(tp), HEAD_DIM)
    cosi_p, sini_p = _rope_tables(jnp.arange(tp), D_IDX)
    pos_s = jnp.full((rs,), past, I32)
    cos_s, sin_s = _rope_tables(pos_s, HEAD_DIM)
    cosi_s, sini_s = _rope_tables(pos_s, D_IDX)

    xp = x_prompt
    xs = jnp.zeros((1, rs, d), F32).at[0, :db].set(x_sample[:, 0])
    tm_p = _row_tile(tp, 512)

    hp = _norm_mod_call(xp, g_mix[0], mod_p[0, :, :, 1], mod_p[0, :, :, 0], tm_p)
    hs = _norm_mod_call(xs, g_mix[0], mod_s[0, :, :, 1], mod_s[0, :, :, 0], rs)

    outs = {n: [] for n in ("kp", "vp", "kip", "cp", "ks", "vs", "kis", "cs")}
    yp = ys = None
    for l in range(depth):
        wl = w_in[l]
        w3 = jnp.stack([wl[:, o_bg:o_cg], wl[:, o_cg:o_vc], wl[:, o_vc:o_q]]).astype(BF16)
        w_q = wl[:, o_q:o_k].astype(BF16)
        w_k = wl[:, o_k:o_v].astype(BF16)
        w_v = wl[:, o_v:o_qi].astype(BF16)
        w_qi = wl[:, o_qi:o_ki].astype(BF16)
        w_kw = jnp.zeros((d, LANES), BF16).at[:, :D_IDX + N_IDX_HEADS].set(wl[:, o_ki:o_ga].astype(BF16))
        w_ga = wl[:, o_ga:o_gb].astype(BF16)
        w_gb = wl[:, o_gb:].astype(BF16)
        wpa, wpb, wo = w_pa[l].astype(BF16), w_pb[l].astype(BF16), w_o[l].astype(BF16)
        wg, wu, wd = w_gate[l].astype(BF16), w_up[l].astype(BF16), w_down[l].astype(BF16)
        last = l == depth - 1
        g_next = g_final if last else g_mix[l + 1]
        ln = l if last else l + 1
        tk_ffn = wd.shape[0] // 4

        sh1, sc1, gt1, sh2, sc2, gt2 = (mod_p[l, :, :, j] for j in range(6))
        a_in, conv_st = _conv_seq_call(hp, w3, conv_w[l], jnp.zeros((bp, CONV_W - 1, d_conv), F32), tm_p, 512)
        q = _proj_call(hp, w_q, cos_p, sin_p, tm_p, half=HEAD_DIM // 2, emit_bf16=True)[0]
        k32, k16 = _proj_call(hp, w_k, cos_p, sin_p, tm_p, half=HEAD_DIM // 2, emit_f32=True, emit_bf16=True)
        v32, v16 = _proj_call(hp, w_v, cos_p, sin_p, tm_p, half=0, emit_f32=True, emit_bf16=True)
        qi_hm = _proj_call(hp, w_qi, cosi_p, sini_p, tm_p, half=D_IDX // 2, head_major=True)[0]
        kw, ki16 = _kidx_call(hp, w_kw, cosi_p, sini_p, tm_p)
        att = _dsa_call(q, qi_hm, kw, k16, v16, ki16, topk_p)
        merged = _merge_call(hp, a_in, att, w_ga, w_gb, wpa, wpb, tm_p, 512)
        xp, h2 = _resid_norm_call(xp, merged, wo, gt1, g_ffn[l], sc2, sh2, min(tm_p, 256))
        act = _ffn_up_call(h2, wg, wu, tm_p, 512)
        xp, nxt = _ffn_down_call(xp, act, wd, gt2, g_next, mod_p[ln, :, :, 1], mod_p[ln, :, :, 0],
                                 tm_p, tk_ffn, last)
        hp, yp = (None, nxt) if last else (nxt, None)
        outs["kp"].append(k32.reshape(bp, tp // PAGE_SIZE, PAGE_SIZE, N_HEADS, HEAD_DIM))
        outs["vp"].append(v32.reshape(bp, tp // PAGE_SIZE, PAGE_SIZE, N_HEADS, HEAD_DIM))
        outs["kip"].append(kw[:, :, :D_IDX].reshape(bp, tp // PAGE_SIZE, PAGE_SIZE, D_IDX))
        outs["cp"].append(conv_st)

        sh1, sc1, gt1, sh2, sc2, gt2 = (mod_s[l, :, :, j] for j in range(6))
        prev = jnp.zeros((rs, CONV_W - 1, d_conv), F32).at[:db].set(state_conv[l])
        a_in, u = _conv_step_call(hs[0], w3, conv_w[l], prev[:, 0], prev[:, 1], 512)
        q32 = _proj_call(hs, w_q, cos_s, sin_s, rs, half=HEAD_DIM // 2, emit_f32=True)[0]
        k32 = _proj_call(hs, w_k, cos_s, sin_s, rs, half=HEAD_DIM // 2, emit_f32=True)[0]
        v32 = _proj_call(hs, w_v, cos_s, sin_s, rs, half=0, emit_f32=True)[0]
        qi_hm = _proj_call(hs, w_qi, cosi_s, sini_s, rs, half=D_IDX // 2, head_major=True)[0]
        kw, ki16 = _kidx_call(hs, w_kw, cosi_s, sini_s, rs)
        qi_hb = qi_hm[0, :, :db].reshape(N_IDX_HEADS * db, D_IDX)
        w_hb = kw[0, :db, D_IDX:D_IDX + N_IDX_HEADS].T.reshape(N_IDX_HEADS * db, 1)
        scores = _pool_scores_call(qi_hb, w_hb, cache_kidx, l, db, _pages_per_step(cache_kidx.shape[1]))
        idx = _select_call(page_table, scores.reshape(db, -1, PAGE_SIZE), qi_hb, w_hb, ki16[0, :db], topk_s)
        att = _gather_attend_call(idx.reshape(db, topk_s), page_table,
                                  q32[0, :db].reshape(db, N_HEADS, HEAD_DIM),
                                  k32[0, :db].reshape(db, N_HEADS, HEAD_DIM),
                                  v32[0, :db].reshape(db, N_HEADS, HEAD_DIM),
                                  cache_k, cache_v, l, topk_s)
        att = jnp.zeros((1, rs, att_dim), BF16).at[0, :db].set(att.reshape(db, att_dim).astype(BF16))
        merged = _merge_call(hs, a_in[None], att, w_ga, w_gb, wpa, wpb, rs, 512)
        xs, h2 = _resid_norm_call(xs, merged, wo, gt1, g_ffn[l], sc2, sh2, rs)
        act = _ffn_up_call(h2, wg, wu, rs, 512)
        xs, nxt = _ffn_down_call(xs, act, wd, gt2, g_next, mod_s[ln, :, :, 1], mod_s[ln, :, :, 0], rs, tk_ffn, last)
        hs, ys = (None, nxt) if last else (nxt, None)
        outs["ks"].append(k32[0, :db].reshape(db, ts, N_HEADS, HEAD_DIM))
        outs["vs"].append(v32[0, :db].reshape(db, ts, N_HEADS, HEAD_DIM))
        outs["kis"].append(kw[0, :db, :D_IDX].reshape(db, ts, D_IDX))
        outs["cs"].append(jnp.stack([prev[:db, 1], u[:db]], axis=1))

    return (yp, ys[0, :db].reshape(db, ts, d),
            jnp.stack(outs["kp"]), jnp.stack(outs["vp"]), jnp.stack(outs["kip"]), jnp.stack(outs["cp"]),
            jnp.stack(outs["ks"]), jnp.stack(outs["vs"]), jnp.stack(outs["kis"]), jnp.stack(outs["cs"]))
```

```python
import functools

import jax
import jax.numpy as jnp
import numpy as np
from jax import lax
from jax.experimental import pallas as pl
from jax.experimental.pallas import tpu as pltpu

F32 = jnp.float32
BF16 = jnp.bfloat16
I32 = jnp.int32
I16 = jnp.int16

N_HEADS = 8
HEAD_DIM = 128
N_IDX_HEADS = 16
D_IDX = 64
CONV_W = 3
TOPK_MAX = 256
PAGE_SIZE = 128
ROPE_THETA = 10000.0
EPS = 1e-6
INDEX_SCALE = (D_IDX ** -0.5) * (N_IDX_HEADS ** -0.5)
ATT_SCALE = HEAD_DIM ** -0.5
EXP2_SCALE = ATT_SCALE * 1.4426950408889634

LANES = 128
SUBLANES = 8
VMEM_LIMIT = 56 * 1024 * 1024
INT_MIN = -2 ** 31
HALF16 = 2 ** 15
M_FLOOR = -1e30
ATT_BLOCK = 256


def _params(sem):
    return pltpu.CompilerParams(dimension_semantics=sem, vmem_limit_bytes=VMEM_LIMIT)


def _dot(a, b):
    return jnp.dot(a, b, preferred_element_type=F32)


def _dot_nt(a, b):
    return lax.dot_general(a, b, (((1,), (1,)), ((), ())), preferred_element_type=F32)


def _rmsnorm(x):
    return x * lax.rsqrt(jnp.mean(x * x, axis=-1, keepdims=True) + EPS)


def _sort_key(s):
    s = jnp.where(s == 0.0, 0.0, s)
    b = pltpu.bitcast(s, I32)
    return jnp.where(b < 0, b ^ 0x7FFFFFFF, b)


def _w_spec(k, tn, layer, col0=0, col_axis=None, row_axis=None):
    def index(*ids):
        return (layer, 0 if row_axis is None else ids[row_axis], col0 + (0 if col_axis is None else ids[col_axis]))
    return pl.BlockSpec((None, k, tn), index)


def _mod_kernel(c_ref, w_ref, b_ref, o_ref):
    c = c_ref[...]
    a = (c * jax.nn.sigmoid(c)).astype(BF16)
    o_ref[...] = _dot(a, w_ref[...].astype(BF16)) + b_ref[...]


def _mod_call(c_all, w_mod, b_mod):
    depth, d, n = w_mod.shape
    rows = c_all.shape[0]
    tn = 1024
    return pl.pallas_call(
        _mod_kernel,
        grid=(depth, n // tn),
        in_specs=[pl.BlockSpec((rows, d), lambda l, j: (0, 0)),
                  pl.BlockSpec((None, d, tn), lambda l, j: (l, 0, j)),
                  pl.BlockSpec((None, 1, tn), lambda l, j: (l, 0, j))],
        out_specs=pl.BlockSpec((None, rows, tn), lambda l, j: (l, 0, j)),
        out_shape=jax.ShapeDtypeStruct((depth, rows, n), F32),
        compiler_params=_params(("arbitrary", "arbitrary")),
        name="mod",
    )(c_all, w_mod, b_mod.reshape(depth, 1, n))


def _mod_spec(mod, tm, row_axis):
    rm, d = mod.shape[1:]
    if rm == 1:
        return pl.BlockSpec((None, 1, d), lambda *ids: (ids[0], 0, 0))
    return pl.BlockSpec((None, tm, d), lambda *ids: (ids[0], ids[row_axis], 0))


def _norm_mod_kernel(x_ref, g_ref, sc_ref, sh_ref, o_ref):
    y = _rmsnorm(x_ref[...]) * g_ref[...]
    o_ref[...] = (y * (1.0 + sc_ref[...]) + sh_ref[...]).astype(o_ref.dtype)


def _norm_mod_call(x, g, sc, sh, tm):
    gn, r, d = x.shape
    return pl.pallas_call(
        _norm_mod_kernel,
        grid=(gn, r // tm),
        in_specs=[pl.BlockSpec((None, tm, d), lambda b, i: (b, i, 0)),
                  pl.BlockSpec((1, d), lambda b, i: (0, 0)),
                  _mod_spec(sc, tm, 1), _mod_spec(sh, tm, 1)],
        out_specs=pl.BlockSpec((None, tm, d), lambda b, i: (b, i, 0)),
        out_shape=jax.ShapeDtypeStruct((gn, r, d), BF16),
        compiler_params=_params(("arbitrary", "arbitrary")),
        name="norm_mod",
    )(x, g.reshape(1, d), sc, sh)


def _conv_seq_kernel(h_ref, wb_ref, wc_ref, wv_ref, cw_ref, prev_ref, a_ref, st_ref, buf, *, tm):
    i = pl.program_id(2)
    h = h_ref[...]
    bg = _dot(h, wb_ref[...])
    u = _dot(h, wc_ref[...]) * _dot(h, wv_ref[...])

    @pl.when(i == 0)
    def _():
        buf[6:8, :] = prev_ref[...]

    buf[8:8 + tm, :] = u
    cw = cw_ref[...]
    y = cw[0:1] * buf[6:6 + tm, :] + cw[1:2] * buf[7:7 + tm, :] + cw[2:3] * u
    a_ref[...] = (bg * y).astype(BF16)
    last2 = buf[6 + tm:8 + tm, :]
    buf[6:8, :] = last2
    st_ref[...] = last2


def _conv_seq_call(h, w_in, layer, cw, prev, tm, tn):
    gn, r, d = h.shape
    n = cw.shape[1]
    nj = n // tn
    return pl.pallas_call(
        functools.partial(_conv_seq_kernel, tm=tm),
        grid=(gn, nj, r // tm),
        in_specs=[pl.BlockSpec((None, tm, d), lambda b, j, i: (b, i, 0)),
                  _w_spec(d, tn, layer, 0, 1), _w_spec(d, tn, layer, nj, 1), _w_spec(d, tn, layer, 2 * nj, 1),
                  pl.BlockSpec((CONV_W, tn), lambda b, j, i: (0, j)),
                  pl.BlockSpec((None, CONV_W - 1, tn), lambda b, j, i: (b, 0, j))],
        out_specs=[pl.BlockSpec((None, tm, tn), lambda b, j, i: (b, i, j)),
                   pl.BlockSpec((None, CONV_W - 1, tn), lambda b, j, i: (b, 0, j))],
        out_shape=[jax.ShapeDtypeStruct((gn, r, n), BF16),
                   jax.ShapeDtypeStruct((gn, CONV_W - 1, n), F32)],
        scratch_shapes=[pltpu.VMEM((8 + tm, tn), F32)],
        compiler_params=_params(("arbitrary", "arbitrary", "arbitrary")),
        name="conv_seq",
    )(h, w_in, w_in, w_in, cw, prev)


def _conv_step_kernel(h_ref, wb_ref, wc_ref, wv_ref, cw_ref, p0_ref, p1_ref, a_ref, u_ref):
    h = h_ref[...]
    bg = _dot(h, wb_ref[...])
    u = _dot(h, wc_ref[...]) * _dot(h, wv_ref[...])
    cw = cw_ref[...]
    y = cw[0:1] * p0_ref[...] + cw[1:2] * p1_ref[...] + cw[2:3] * u
    a_ref[...] = (bg * y).astype(BF16)
    u_ref[...] = u


def _conv_step_call(h, w_in, layer, cw, p0, p1, tn):
    r, d = h.shape
    n = cw.shape[1]
    nj = n // tn
    row_tile = pl.BlockSpec((r, tn), lambda j: (0, j))
    return pl.pallas_call(
        _conv_step_kernel,
        grid=(nj,),
        in_specs=[pl.BlockSpec((r, d), lambda j: (0, 0)),
                  _w_spec(d, tn, layer, 0, 0), _w_spec(d, tn, layer, nj, 0), _w_spec(d, tn, layer, 2 * nj, 0),
                  pl.BlockSpec((CONV_W, tn), lambda j: (0, j)),
                  row_tile, row_tile],
        out_specs=[row_tile, row_tile],
        out_shape=[jax.ShapeDtypeStruct((r, n), BF16), jax.ShapeDtypeStruct((r, n), F32)],
        compiler_params=_params(("arbitrary",)),
        name="conv_step",
    )(h, w_in, w_in, w_in, cw, p0, p1)


def _rope_group(x, cos, sin, half):
    if 2 * half == LANES:
        rot = pltpu.roll(x, half, 1)
    else:
        lane = lax.broadcasted_iota(I32, x.shape, 1)
        rot = jnp.where(lane % (2 * half) < half, pltpu.roll(x, LANES - half, 1), pltpu.roll(x, half, 1))
    return x * cos + rot * sin


def _proj_kernel(*refs, half, outs, tck, aliased):
    h_ref, w_ref, cos_ref, sin_ref = refs[:4]
    out_refs = list(refs[4 + aliased:])
    acc = _dot(h_ref[...], w_ref[...])
    cos, sin = cos_ref[...], sin_ref[...]
    groups = []
    for c in range(acc.shape[1] // LANES):
        x = acc[:, c * LANES:(c + 1) * LANES]
        groups.append(_rope_group(x, cos, sin, half) if half else x)
    for kind in outs:
        o = out_refs.pop(0)
        for c, x in enumerate(groups):
            if kind == "f32":
                o[:, c * LANES:(c + 1) * LANES] = x
            elif kind == "bf16":
                o[:, c * LANES:(c + 1) * LANES] = x.astype(BF16)
            elif kind == "heads":
                o[:, c, :] = x
            elif kind == "idx_heads":
                xb = x.astype(BF16)
                o[2 * c] = xb[:, :D_IDX]
                o[2 * c + 1] = xb[:, D_IDX:]
            elif kind == "chunk_t":
                for k in range(x.shape[0] // tck):
                    o[k, c * LANES:(c + 1) * LANES, :] = x[k * tck:(k + 1) * tck, :].T.astype(BF16)


def _proj_call(h, w_in, layer, col0, n, cos, sin, tm, *, half, outs, tck=None, cache=None):
    gn, r, d = h.shape
    depth = None if cache is None else cache.shape[0]
    heads = n // LANES
    out_specs, out_shape = [], []
    for kind in outs:
        if kind in ("f32", "bf16"):
            out_specs.append(pl.BlockSpec((None, tm, n), lambda b, i: (b, i, 0)))
            out_shape.append(jax.ShapeDtypeStruct((gn, r, n), F32 if kind == "f32" else BF16))
        elif kind == "heads":
            out_specs.append(pl.BlockSpec((None, None, tm, heads, LANES), lambda b, i: (layer, b, i, 0, 0)))
            out_shape.append(jax.ShapeDtypeStruct((depth, gn, r, heads, LANES), F32))
        elif kind == "idx_heads":
            out_specs.append(pl.BlockSpec((None, n // D_IDX, tm, D_IDX), lambda b, i: (b, 0, i, 0)))
            out_shape.append(jax.ShapeDtypeStruct((gn, n // D_IDX, r, D_IDX), BF16))
        elif kind == "chunk_t":
            out_specs.append(pl.BlockSpec((None, tm // tck, n, tck), lambda b, i: (b, i, 0, 0)))
            out_shape.append(jax.ShapeDtypeStruct((gn, r // tck, n, tck), BF16))
    in_specs = [pl.BlockSpec((None, tm, d), lambda b, i: (b, i, 0)),
                _w_spec(d, n, layer, col0),
                pl.BlockSpec((tm, LANES), lambda b, i: (i, 0)),
                pl.BlockSpec((tm, LANES), lambda b, i: (i, 0))]
    args = [h, w_in, cos, sin]
    aliases = {}
    if cache is not None:
        in_specs.append(pl.BlockSpec(memory_space=pl.ANY))
        args.append(cache)
        aliases = {4: outs.index("heads")}
    return pl.pallas_call(
        functools.partial(_proj_kernel, half=half, outs=outs, tck=tck, aliased=cache is not None),
        grid=(gn, r // tm),
        in_specs=in_specs, out_specs=out_specs, out_shape=out_shape,
        input_output_aliases=aliases,
        compiler_params=_params(("arbitrary", "arbitrary")),
        name="proj",
    )(*args)


def _kidx_kernel(h_ref, w_ref, cos_ref, sin_ref, kw_ref, kb_ref, wt_ref):
    acc = _dot(h_ref[...], w_ref[...])
    roped = _rope_group(acc, cos_ref[...], sin_ref[...], D_IDX // 2)
    lane = lax.broadcasted_iota(I32, acc.shape, 1)
    kw = jnp.where(lane < D_IDX, roped, jnp.where(lane < D_IDX + N_IDX_HEADS, acc, 0.0))
    kw_ref[...] = kw
    kb_ref[...] = kw[:, :D_IDX].astype(BF16)
    wt_ref[...] = kw.T[D_IDX:D_IDX + N_IDX_HEADS, :]


def _kidx_call(h, w_in, layer, col0, cos, sin, tm):
    gn, r, d = h.shape
    return pl.pallas_call(
        _kidx_kernel,
        grid=(gn, r // tm),
        in_specs=[pl.BlockSpec((None, tm, d), lambda b, i: (b, i, 0)),
                  _w_spec(d, LANES, layer, col0),
                  pl.BlockSpec((tm, LANES), lambda b, i: (i, 0)),
                  pl.BlockSpec((tm, LANES), lambda b, i: (i, 0))],
        out_specs=[pl.BlockSpec((None, tm, LANES), lambda b, i: (b, i, 0)),
                   pl.BlockSpec((None, tm, D_IDX), lambda b, i: (b, i, 0)),
                   pl.BlockSpec((None, N_IDX_HEADS, tm), lambda b, i: (b, 0, i))],
        out_shape=[jax.ShapeDtypeStruct((gn, r, LANES), F32), jax.ShapeDtypeStruct((gn, r, D_IDX), BF16),
                   jax.ShapeDtypeStruct((gn, N_IDX_HEADS, r), F32)],
        compiler_params=_params(("arbitrary", "arbitrary")),
        name="kidx",
    )(h, w_in, cos, sin)


def _strict_tri(n, dtype, lower):
    row = lax.broadcasted_iota(I32, (n, n), 0)
    col = lax.broadcasted_iota(I32, (n, n), 1)
    return jnp.where(col < row if lower else row < col, 1.0, 0.0).astype(dtype)


def _dsa_kernel(q_ref, qi_ref, wi_ref, k_ref, vt_ref, ki_ref, o_ref,
                key_scr, hi_scr, lo_scr, m_scr, l_scr, acc_scr, *, topk, ck, qb):
    i = pl.program_id(1)
    nch = ((i + 1) * qb + ck - 1) // ck
    qpos = i * qb + lax.broadcasted_iota(I32, (LANES, qb), 1)
    krow = lax.broadcasted_iota(I32, (LANES, qb), 0)
    w = wi_ref[...]

    def index_chunk(c, _):
        base = pl.multiple_of(c * ck, ck)
        for r in range(ck // LANES):
            rows = slice(r * LANES, (r + 1) * LANES)
            kc = ki_ref[pl.ds(base + r * LANES, LANES), :]
            acc = jnp.zeros((LANES, qb), F32)
            for h in range(N_IDX_HEADS):
                acc = acc + jnp.maximum(_dot_nt(kc, qi_ref[h]), 0.0) * w[h:h + 1, :]
            kpos = base + r * LANES + krow
            key = jnp.where(kpos <= qpos, _sort_key(acc * INDEX_SCALE), INT_MIN)
            key_scr[c, rows, :] = key
            hi_scr[c, rows, :] = lax.shift_right_arithmetic(key, 16).astype(I16)
            lo_scr[c, rows, :] = ((key & 0xFFFF) - HALF16).astype(I16)
        return 0

    lax.fori_loop(0, nch, index_chunk, 0)

    ge = lambda a, b: a >= b
    gt = lambda a, b: a > b
    pack = 2 * SUBLANES

    def count16(scr, pred, thr):
        thr = thr.astype(I16)

        def body(c, tot):
            hit = jnp.where(pred(scr[c], thr), jnp.ones((), BF16), jnp.zeros((), BF16))
            part = hit[0:pack]
            for k in range(1, ck // pack):
                part = part + hit[k * pack:(k + 1) * pack]
            return tot + part.astype(F32)

        tot = lax.fori_loop(0, nch, body, jnp.zeros((pack, qb), F32))
        return jnp.sum(tot, axis=0, keepdims=True).astype(I32)

    def search16(scr, need):
        def step(it, t):
            cand = t + jnp.left_shift(jnp.int32(1), 15 - it)
            return jnp.where(count16(scr, ge, cand) >= need, cand, t)
        return lax.fori_loop(0, 16, step, jnp.full((1, qb), -HALF16, I32))

    t_hi = search16(hi_scr, topk)
    need_lo = topk - count16(hi_scr, gt, t_hi)
    t_hi16 = t_hi.astype(I16)

    def keep_band(c, _):
        lo_scr[c] = jnp.where(hi_scr[c] == t_hi16, lo_scr[c], jnp.full((), -HALF16, I16))
        return 0

    lax.fori_loop(0, nch, keep_band, 0)
    t_lo = search16(lo_scr, need_lo)
    t = t_hi * (2 * HALF16) + (t_lo + HALF16)
    t = jnp.maximum(t, INT_MIN + 1)

    def count(pred, thr):
        def body(c, cnt):
            hit = jnp.where(pred(key_scr[c], thr), 1, 0)
            return cnt + jnp.sum(hit.reshape(ck // SUBLANES, SUBLANES, qb), axis=0)
        cnt = lax.fori_loop(0, nch, body, jnp.zeros((SUBLANES, qb), I32))
        return jnp.sum(cnt, axis=0, keepdims=True)

    c_ge = count(ge, t)

    @pl.when(jnp.max(c_ge) > topk)
    def _():
        need = (topk - count(gt, t)).astype(F32)
        lower = _strict_tri(ck, BF16, lower=True)

        def body(c, carry):
            kc = key_scr[c]
            eq = kc == t
            eqf = jnp.where(eq, 1.0, 0.0)
            before = _dot(lower, eqf.astype(BF16)) + carry
            key_scr[c] = jnp.where(eq & (before >= need), INT_MIN, kc)
            return carry + jnp.sum(eqf, axis=0, keepdims=True)

        lax.fori_loop(0, nch, body, jnp.zeros((1, qb), F32))

    m_scr[...] = jnp.full(m_scr.shape, M_FLOOR, F32)
    l_scr[...] = jnp.zeros(l_scr.shape, F32)
    acc_scr[...] = jnp.zeros(acc_scr.shape, F32)

    def attend_chunk(c, _):
        base = pl.multiple_of(c * ck, ck)
        sel = key_scr[c] >= t
        heads = [slice(h * HEAD_DIM, (h + 1) * HEAD_DIM) for h in range(N_HEADS)]
        raw = [_dot_nt(k_ref[pl.ds(base, ck), cols], q_ref[:, cols]) for cols in heads]
        probs = []
        for h in range(N_HEADS):
            s = jnp.where(sel, raw[h], -jnp.inf)
            m_old = m_scr[h]
            m_new = jnp.maximum(m_old, jnp.max(s, axis=0, keepdims=True))
            alpha = jnp.exp2((m_old - m_new) * EXP2_SCALE)
            p = jnp.exp2((s - m_new) * EXP2_SCALE)
            l_scr[h] = alpha * l_scr[h] + jnp.sum(p, axis=0, keepdims=True)
            m_scr[h] = m_new
            probs.append((alpha, p.astype(BF16)))
        for h in range(N_HEADS):
            alpha, p = probs[h]
            acc_scr[h] = alpha * acc_scr[h] + _dot(vt_ref[c, heads[h], :], p)
        return 0

    lax.fori_loop(0, nch, attend_chunk, 0)
    for h in range(N_HEADS):
        o_ref[:, h * HEAD_DIM:(h + 1) * HEAD_DIM] = (acc_scr[h] / l_scr[h]).T.astype(o_ref.dtype)


def _dsa_call(q, qi_hm, wi_t, k, v_t, ki, topk, blk):
    bn, t, a = q.shape
    full = lambda shape: pl.BlockSpec((None,) + shape, lambda b, i: (b,) + (0,) * len(shape))
    return pl.pallas_call(
        functools.partial(_dsa_kernel, topk=topk, ck=blk, qb=blk),
        grid=(bn, t // blk),
        in_specs=[pl.BlockSpec((None, blk, a), lambda b, i: (b, i, 0)),
                  pl.BlockSpec((None, N_IDX_HEADS, blk, D_IDX), lambda b, i: (b, 0, i, 0)),
                  pl.BlockSpec((None, N_IDX_HEADS, blk), lambda b, i: (b, 0, i)),
                  full((t, a)), full((t // blk, a, blk)), full((t, D_IDX))],
        out_specs=pl.BlockSpec((None, blk, a), lambda b, i: (b, i, 0)),
        out_shape=jax.ShapeDtypeStruct((bn, t, a), BF16),
        scratch_shapes=[pltpu.VMEM((t // blk, blk, blk), I32),
                        pltpu.VMEM((t // blk, blk, blk), I16),
                        pltpu.VMEM((t // blk, blk, blk), I16),
                        pltpu.VMEM((N_HEADS, 1, blk), F32),
                        pltpu.VMEM((N_HEADS, 1, blk), F32),
                        pltpu.VMEM((N_HEADS, HEAD_DIM, blk), F32)],
        compiler_params=_params(("arbitrary", "arbitrary")),
        name="dsa_prompt",
    )(q, qi_hm, wi_t, k, v_t, ki)


def _pool_scores_kernel(qi_ref, w_ref, ck_ref, o_ref, *, nb):
    pages, page, di = ck_ref.shape
    kc = ck_ref[...].reshape(pages * page, di).astype(BF16)
    d = jnp.maximum(_dot_nt(qi_ref[...], kc), 0.0) * w_ref[...]
    s = d[0:nb]
    for h in range(1, N_IDX_HEADS):
        s = s + d[h * nb:(h + 1) * nb]
    o_ref[...] = s * INDEX_SCALE


def _pool_scores_call(qi_hb, w_hb, cache_kidx, layer, nb, pages_per_step):
    depth, n_pool, page, di = cache_kidx.shape
    return pl.pallas_call(
        functools.partial(_pool_scores_kernel, nb=nb),
        grid=(n_pool // pages_per_step,),
        in_specs=[pl.BlockSpec(qi_hb.shape, lambda j: (0, 0)),
                  pl.BlockSpec(w_hb.shape, lambda j: (0, 0)),
                  pl.BlockSpec((None, pages_per_step, page, di), lambda j: (layer, j, 0, 0))],
        out_specs=pl.BlockSpec((nb, pages_per_step * page), lambda j: (0, j)),
        out_shape=jax.ShapeDtypeStruct((nb, n_pool * page), F32),
        compiler_params=_params(("arbitrary",)),
        name="pool_scores",
    )(qi_hb, w_hb, cache_kidx)


def _select_kernel(pt_ref, sc_ref, qi_ref, w_ref, kn_ref, idx_ref, s_scr, rk_scr, *, topk, n_pages, nb):
    b = pl.program_id(0)
    past = n_pages * PAGE_SIZE

    def gather_page(p, _):
        s_scr[pl.ds(p, 1), :] = sc_ref[pl.ds(pt_ref[b, p], 1), :]
        return 0

    lax.fori_loop(0, n_pages, gather_page, 0)
    key = _sort_key(s_scr[...])

    d = jnp.maximum(_dot_nt(qi_ref[...], kn_ref[...]), 0.0) * w_ref[...]
    row = lax.broadcasted_iota(I32, d.shape, 0)
    col = lax.broadcasted_iota(I32, d.shape, 1)
    mine = jnp.where((row % nb == b) & (col == b), d, 0.0)
    s_new = jnp.sum(jnp.sum(mine, axis=1, keepdims=True), axis=0, keepdims=True) * INDEX_SCALE
    key_new = _sort_key(s_new)

    def total(x):
        return jnp.sum(jnp.sum(x, axis=1, keepdims=True), axis=0, keepdims=True)

    def count_ge(thr):
        return total(jnp.where(key >= thr, 1, 0)) + jnp.where(key_new >= thr, 1, 0)

    def search(it, t):
        cand = t + jnp.left_shift(jnp.int32(1), 31 - it)
        return jnp.where(count_ge(cand) >= topk, cand, t)

    t = lax.fori_loop(0, 32, search, jnp.full((1, 1), INT_MIN, I32))
    t = jnp.maximum(t, INT_MIN + 1)

    upper = _strict_tri(PAGE_SIZE, BF16, lower=False)
    lower = _strict_tri(n_pages, BF16, lower=True)

    def ranks(flags):
        f = jnp.where(flags, 1.0, 0.0)
        rows = jnp.broadcast_to(jnp.sum(f, axis=1, keepdims=True), f.shape)
        return _dot(f.astype(BF16), upper) + _dot(lower, rows.astype(BF16)), total(f)

    n_gt = total(jnp.where(key > t, 1.0, 0.0)) + jnp.where(key_new > t, 1.0, 0.0)
    need = topk - n_gt
    eq = key == t
    eq_before, n_eq = ranks(eq)
    sel = (key > t) | (eq & (eq_before < need))
    sel_new = (key_new > t) | ((key_new == t) & (n_eq < need))

    rank, n_sel = ranks(sel)
    rk_scr[...] = jnp.where(sel, rank, -1.0)
    slot = lax.broadcasted_iota(I32, (topk, PAGE_SIZE), 0).astype(F32)
    off = lax.broadcasted_iota(I32, (topk, PAGE_SIZE), 1)

    def place(p, acc):
        hit = slot == jnp.broadcast_to(rk_scr[pl.ds(p, 1), :], slot.shape)
        return acc + jnp.where(hit, p * PAGE_SIZE + off, 0)

    acc = lax.fori_loop(0, n_pages, place, jnp.zeros((topk, PAGE_SIZE), I32))
    idx = jnp.sum(acc, axis=1, keepdims=True)
    idx_ref[...] = idx + jnp.where(sel_new & (slot[:, 0:1] == n_sel), past, 0)


def _select_call(page_table, scores, qi_hb, w_hb, k_new, topk):
    nb, n_pages = page_table.shape
    n_pool = scores.shape[1]
    return pl.pallas_call(
        functools.partial(_select_kernel, topk=topk, n_pages=n_pages, nb=nb),
        grid_spec=pltpu.PrefetchScalarGridSpec(
            num_scalar_prefetch=1, grid=(nb,),
            in_specs=[pl.BlockSpec((None, n_pool, PAGE_SIZE), lambda b, pt: (b, 0, 0)),
                      pl.BlockSpec(qi_hb.shape, lambda b, pt: (0, 0)),
                      pl.BlockSpec(w_hb.shape, lambda b, pt: (0, 0)),
                      pl.BlockSpec(k_new.shape, lambda b, pt: (0, 0))],
            out_specs=pl.BlockSpec((None, topk, 1), lambda b, pt: (b, 0, 0)),
            scratch_shapes=[pltpu.VMEM((n_pages, PAGE_SIZE), F32), pltpu.VMEM((n_pages, PAGE_SIZE), F32)]),
        out_shape=jax.ShapeDtypeStruct((nb, topk, 1), I32),
        compiler_params=_params(("arbitrary",)),
        name="select",
    )(page_table, scores, qi_hb, w_hb, k_new)


def _gather_attend_kernel(idx_ref, pt_ref, q_ref, kn_hbm, vn_hbm, ck_hbm, cv_hbm, o_ref,
                          kbuf, vbuf, sem, *, layer, topk, n_pages):
    b = pl.program_id(0)
    past = n_pages * PAGE_SIZE

    def new_copies(r):
        return (pltpu.make_async_copy(kn_hbm.at[b], kbuf.at[r], sem.at[0]),
                pltpu.make_async_copy(vn_hbm.at[b], vbuf.at[r], sem.at[1]))

    def start(r, _):
        pos = idx_ref[b, r]

        @pl.when(pos < past)
        def _():
            phys = pt_ref[b, pos // PAGE_SIZE]
            off = pos % PAGE_SIZE
            pltpu.make_async_copy(ck_hbm.at[layer, phys, off], kbuf.at[r], sem.at[0]).start()
            pltpu.make_async_copy(cv_hbm.at[layer, phys, off], vbuf.at[r], sem.at[1]).start()

        @pl.when(pos >= past)
        def _():
            for cp in new_copies(r):
                cp.start()
        return 0

    def wait(r, _):
        for cp in new_copies(r):
            cp.wait()
        return 0

    lax.fori_loop(0, topk, start, 0)
    lax.fori_loop(0, topk, wait, 0)

    q = q_ref[...]
    s = jnp.sum(kbuf[...] * q[None], axis=-1, keepdims=True) * ATT_SCALE
    p = jnp.exp(s - jnp.max(s, axis=0, keepdims=True))
    p = p / jnp.sum(p, axis=0, keepdims=True)
    o_ref[...] = jnp.sum(p * vbuf[...], axis=0)


def _gather_attend_call(idx, page_table, q, k_new, v_new, cache_k, cache_v, layer, topk):
    nb, n_pages = page_table.shape
    any_spec = pl.BlockSpec(memory_space=pl.ANY)
    return pl.pallas_call(
        functools.partial(_gather_attend_kernel, layer=layer, topk=topk, n_pages=n_pages),
        grid_spec=pltpu.PrefetchScalarGridSpec(
            num_scalar_prefetch=2, grid=(nb,),
            in_specs=[pl.BlockSpec((None, N_HEADS, HEAD_DIM), lambda b, idx, pt: (b, 0, 0)),
                      any_spec, any_spec, any_spec, any_spec],
            out_specs=pl.BlockSpec((None, N_HEADS, HEAD_DIM), lambda b, idx, pt: (b, 0, 0)),
            scratch_shapes=[pltpu.VMEM((topk, N_HEADS, HEAD_DIM), F32),
                            pltpu.VMEM((topk, N_HEADS, HEAD_DIM), F32),
                            pltpu.SemaphoreType.DMA((2,))]),
        out_shape=jax.ShapeDtypeStruct((nb, N_HEADS, HEAD_DIM), F32),
        compiler_params=_params(("arbitrary",)),
        name="gather_attend",
    )(idx, page_table, q, k_new, v_new, cache_k, cache_v)


def _merge_kernel(h_ref, a_ref, t_ref, wga_ref, wgb_ref, wpa_ref, wpb_ref, o_ref):
    h = h_ref[...]
    ya = _dot(a_ref[...], wpa_ref[...])
    yb = _dot(t_ref[...], wpb_ref[...])
    ga = jax.nn.sigmoid(_dot(h, wga_ref[...]))
    gb = jax.nn.sigmoid(_dot(h, wgb_ref[...]))
    o_ref[...] = (ga * ya + gb * yb).astype(BF16)


def _merge_call(h, a_in, att, w_gates, w_pa, w_pb, layer, tm, tn):
    gn, r, d = h.shape
    a = a_in.shape[2]
    rows = lambda w: pl.BlockSpec((None, tm, w), lambda b, j, i: (b, i, 0))
    return pl.pallas_call(
        _merge_kernel,
        grid=(gn, d // tn, r // tm),
        in_specs=[rows(d), rows(a), rows(a),
                  _w_spec(d, tn, layer, 0, 1), _w_spec(d, tn, layer, d // tn, 1),
                  _w_spec(a, tn, layer, 0, 1), _w_spec(a, tn, layer, 0, 1)],
        out_specs=pl.BlockSpec((None, tm, tn), lambda b, j, i: (b, i, j)),
        out_shape=jax.ShapeDtypeStruct((gn, r, d), BF16),
        compiler_params=_params(("arbitrary", "arbitrary", "arbitrary")),
        name="merge",
    )(h, a_in, att, w_gates, w_gates, w_pa, w_pb)


def _resid_norm_kernel(x_ref, m_ref, w_ref, gt_ref, g_ref, sc_ref, sh_ref, x_out, h_out):
    x = x_ref[...] + gt_ref[...] * _dot(m_ref[...], w_ref[...])
    x_out[...] = x
    y = _rmsnorm(x) * g_ref[...]
    h_out[...] = (y * (1.0 + sc_ref[...]) + sh_ref[...]).astype(BF16)


def _resid_norm_call(x, m, w_o, layer, gt, g, sc, sh, tm):
    gn, r, d = x.shape
    row = lambda: pl.BlockSpec((None, tm, d), lambda b, i: (b, i, 0))
    return pl.pallas_call(
        _resid_norm_kernel,
        grid=(gn, r // tm),
        in_specs=[row(), row(), _w_spec(d, d, layer),
                  _mod_spec(gt, tm, 1), pl.BlockSpec((1, d), lambda b, i: (0, 0)),
                  _mod_spec(sc, tm, 1), _mod_spec(sh, tm, 1)],
        out_specs=[row(), row()],
        out_shape=[jax.ShapeDtypeStruct((gn, r, d), F32), jax.ShapeDtypeStruct((gn, r, d), BF16)],
        compiler_params=_params(("arbitrary", "arbitrary")),
        name="resid_norm",
    )(x, m, w_o, gt, g.reshape(1, d), sc, sh)


def _ffn_up_kernel(h_ref, wg_ref, wu_ref, o_ref):
    h = h_ref[...]
    g = _dot(h, wg_ref[...])
    o_ref[...] = (g * jax.nn.sigmoid(g) * _dot(h, wu_ref[...])).astype(BF16)


def _ffn_up_call(h, w_gate, w_up, layer, tm, tn):
    gn, r, d = h.shape
    f = w_gate.shape[2]
    return pl.pallas_call(
        _ffn_up_kernel,
        grid=(gn, f // tn, r // tm),
        in_specs=[pl.BlockSpec((None, tm, d), lambda b, j, i: (b, i, 0)),
                  _w_spec(d, tn, layer, 0, 1), _w_spec(d, tn, layer, 0, 1)],
        out_specs=pl.BlockSpec((None, tm, tn), lambda b, j, i: (b, i, j)),
        out_shape=jax.ShapeDtypeStruct((gn, r, f), BF16),
        compiler_params=_params(("arbitrary", "arbitrary", "arbitrary")),
        name="ffn_up",
    )(h, w_gate, w_up)


def _ffn_down_kernel(x_ref, a_ref, w_ref, gt_ref, g_ref, sc_ref, sh_ref, x_out, n_out, acc, *, final):
    k = pl.program_id(2)

    @pl.when(k == 0)
    def _():
        acc[...] = jnp.zeros(acc.shape, F32)

    acc[...] += _dot(a_ref[...], w_ref[...])

    @pl.when(k == pl.num_programs(2) - 1)
    def _():
        x = x_ref[...] + gt_ref[...] * acc[...]
        x_out[...] = x
        y = _rmsnorm(x) * g_ref[...]
        if not final:
            y = y * (1.0 + sc_ref[...]) + sh_ref[...]
        n_out[...] = y.astype(n_out.dtype)


def _ffn_down_call(x, act, w_down, layer, gt, g, sc, sh, tm, tk, final):
    gn, r, d = x.shape
    f = act.shape[2]
    row = lambda: pl.BlockSpec((None, tm, d), lambda b, i, k: (b, i, 0))
    return pl.pallas_call(
        functools.partial(_ffn_down_kernel, final=final),
        grid=(gn, r // tm, f // tk),
        in_specs=[row(), pl.BlockSpec((None, tm, tk), lambda b, i, k: (b, i, k)),
                  _w_spec(tk, d, layer, row_axis=2),
                  _mod_spec(gt, tm, 1), pl.BlockSpec((1, d), lambda b, i, k: (0, 0)),
                  _mod_spec(sc, tm, 1), _mod_spec(sh, tm, 1)],
        out_specs=[row(), row()],
        out_shape=[jax.ShapeDtypeStruct((gn, r, d), F32),
                   jax.ShapeDtypeStruct((gn, r, d), F32 if final else BF16)],
        scratch_shapes=[pltpu.VMEM((tm, d), F32)],
        compiler_params=_params(("arbitrary", "arbitrary", "arbitrary")),
        name="ffn_down",
    )(x, act, w_down, gt, g.reshape(1, d), sc, sh)


def _rope_tables(pos, d):
    inv_freq = ROPE_THETA ** (-np.arange(0, d, 2, dtype=np.float64) / d)
    ang = np.asarray(pos, np.float64)[:, None] * inv_freq[None, :]
    cos, sin = np.cos(ang), np.sin(ang)
    reps = LANES // d
    return (jnp.asarray(np.tile(np.concatenate([cos, cos], axis=1), (1, reps)), F32),
            jnp.asarray(np.tile(np.concatenate([-sin, sin], axis=1), (1, reps)), F32))


def _pages_per_step(n_pool):
    return max(p for p in range(1, 33) if n_pool % p == 0)


def kernel(x_prompt, x_sample, c_prompt, c_sample, cache_k, cache_v, cache_kidx, state_conv, page_table,
           w_mod, b_mod, g_mix, g_ffn, w_in, conv_w, w_pa, w_pb, w_o, w_gate, w_up, w_down, g_final):
    bp, tp, d = x_prompt.shape
    db, ts, _ = x_sample.shape
    assert ts == 1, "the sample path handles one new token per sequence"
    depth = w_mod.shape[0]
    n_pages = page_table.shape[1]
    past = n_pages * PAGE_SIZE
    att_dim = N_HEADS * HEAD_DIM
    d_conv = d // 2
    idx_q = N_IDX_HEADS * D_IDX
    topk_p = min(TOPK_MAX, tp // 4)
    topk_s = min(TOPK_MAX, (past + ts) // 4)
    rs = 16
    blk = ATT_BLOCK if tp % ATT_BLOCK == 0 else LANES
    assert db <= rs and tp % blk == 0 and d_conv == att_dim == idx_q

    cb_q, cb_k, cb_v, cb_qi = 3, 4, 5, 6
    o_ki = 7 * att_dim
    o_ga = o_ki + D_IDX + N_IDX_HEADS
    assert o_ki % LANES == 0

    w_in16 = w_in.astype(BF16)
    w_gates = w_in[:, :, o_ga:].astype(BF16)
    w_pa16, w_pb16, w_o16 = w_pa.astype(BF16), w_pb.astype(BF16), w_o.astype(BF16)
    w_gate16, w_up16, w_down16 = w_gate.astype(BF16), w_up.astype(BF16), w_down.astype(BF16)

    c_all = jnp.zeros((rs, d), F32).at[:bp].set(c_prompt).at[bp:bp + db].set(c_sample)
    mod = _mod_call(c_all, w_mod, b_mod)
    mod_p = mod[:, :bp].reshape(depth, bp, 1, 6, d)
    mod_s = jnp.zeros((depth, 1, rs, 6, d), F32).at[:, 0, :db].set(mod[:, bp:bp + db].reshape(depth, db, 6, d))

    cos_p, sin_p = _rope_tables(np.arange(tp), HEAD_DIM)
    cosi_p, sini_p = _rope_tables(np.arange(tp), D_IDX)
    cos_s, sin_s = _rope_tables(np.full((rs,), past), HEAD_DIM)
    cosi_s, sini_s = _rope_tables(np.full((rs,), past), D_IDX)

    xp = x_prompt
    xs = jnp.zeros((1, rs, d), F32).at[0, :db].set(x_sample[:, 0])
    tm_p = min(tp, 512)
    tk_ffn = w_down.shape[1] // 4

    hp = _norm_mod_call(xp, g_mix[0], mod_p[0, :, :, 1], mod_p[0, :, :, 0], tm_p)
    hs = _norm_mod_call(xs, g_mix[0], mod_s[0, :, :, 1], mod_s[0, :, :, 0], rs)

    outs = {n: [] for n in ("kip", "cp", "ks", "vs", "kis", "cs")}
    kp_all = jnp.zeros((depth, bp, tp, N_HEADS, HEAD_DIM), F32)
    vp_all = jnp.zeros((depth, bp, tp, N_HEADS, HEAD_DIM), F32)
    yp = ys = None
    for l in range(depth):
        last = l == depth - 1
        g_next = g_final if last else g_mix[l + 1]
        ln = l if last else l + 1

        sh1, sc1, gt1, sh2, sc2, gt2 = (mod_p[l, :, :, j] for j in range(6))
        a_in, conv_st = _conv_seq_call(hp, w_in16, l, conv_w[l], jnp.zeros((bp, CONV_W - 1, d_conv), F32), tm_p, 512)
        q = _proj_call(hp, w_in16, l, cb_q, att_dim, cos_p, sin_p, tm_p, half=HEAD_DIM // 2, outs=("bf16",))[0]
        kp_all, k16 = _proj_call(hp, w_in16, l, cb_k, att_dim, cos_p, sin_p, tm_p, half=HEAD_DIM // 2,
                                 outs=("heads", "bf16"), cache=kp_all)
        vp_all, v_t = _proj_call(hp, w_in16, l, cb_v, att_dim, cos_p, sin_p, tm_p, half=0,
                                 outs=("heads", "chunk_t"), tck=blk, cache=vp_all)
        qi_hm = _proj_call(hp, w_in16, l, cb_qi, idx_q, cosi_p, sini_p, tm_p, half=D_IDX // 2, outs=("idx_heads",))[0]
        kw, ki16, wi_t = _kidx_call(hp, w_in16, l, o_ki // LANES, cosi_p, sini_p, tm_p)
        att = _dsa_call(q, qi_hm, wi_t, k16, v_t, ki16, topk_p, blk)
        merged = _merge_call(hp, a_in, att, w_gates, w_pa16, w_pb16, l, tm_p, 512)
        xp, h2 = _resid_norm_call(xp, merged, w_o16, l, gt1, g_ffn[l], sc2, sh2, min(tm_p, 256))
        act = _ffn_up_call(h2, w_gate16, w_up16, l, tm_p, 512)
        xp, nxt = _ffn_down_call(xp, act, w_down16, l, gt2, g_next, mod_p[ln, :, :, 1], mod_p[ln, :, :, 0],
                                 tm_p, tk_ffn, last)
        hp, yp = (None, nxt) if last else (nxt, None)
        outs["kip"].append(kw[:, :, :D_IDX].reshape(bp, tp // PAGE_SIZE, PAGE_SIZE, D_IDX))
        outs["cp"].append(conv_st)

        sh1, sc1, gt1, sh2, sc2, gt2 = (mod_s[l, :, :, j] for j in range(6))
        prev = jnp.zeros((rs, CONV_W - 1, d_conv), F32).at[:db].set(state_conv[l])
        a_in, u = _conv_step_call(hs[0], w_in16, l, conv_w[l], prev[:, 0], prev[:, 1], 512)
        q32 = _proj_call(hs, w_in16, l, cb_q, att_dim, cos_s, sin_s, rs, half=HEAD_DIM // 2, outs=("f32",))[0]
        k32 = _proj_call(hs, w_in16, l, cb_k, att_dim, cos_s, sin_s, rs, half=HEAD_DIM // 2, outs=("f32",))[0]
        v32 = _proj_call(hs, w_in16, l, cb_v, att_dim, cos_s, sin_s, rs, half=0, outs=("f32",))[0]
        qi_hm = _proj_call(hs, w_in16, l, cb_qi, idx_q, cosi_s, sini_s, rs, half=D_IDX // 2, outs=("idx_heads",))[0]
        kw, ki16, _ = _kidx_call(hs, w_in16, l, o_ki // LANES, cosi_s, sini_s, rs)
        qi_hb = qi_hm[0, :, :db].reshape(N_IDX_HEADS * db, D_IDX)
        w_hb = kw[0, :db, D_IDX:D_IDX + N_IDX_HEADS].T.reshape(N_IDX_HEADS * db, 1)
        scores = _pool_scores_call(qi_hb, w_hb, cache_kidx, l, db, _pages_per_step(cache_kidx.shape[1]))
        idx = _select_call(page_table, scores.reshape(db, -1, PAGE_SIZE), qi_hb, w_hb, ki16[0, :db], topk_s)
        att = _gather_attend_call(idx.reshape(db, topk_s), page_table,
                                  q32[0, :db].reshape(db, N_HEADS, HEAD_DIM),
                                  k32[0, :db].reshape(db, N_HEADS, HEAD_DIM),
                                  v32[0, :db].reshape(db, N_HEADS, HEAD_DIM),
                                  cache_k, cache_v, l, topk_s)
        att = jnp.zeros((1, rs, att_dim), BF16).at[0, :db].set(att.reshape(db, att_dim).astype(BF16))
        merged = _merge_call(hs, a_in[None], att, w_gates, w_pa16, w_pb16, l, rs, 512)
        xs, h2 = _resid_norm_call(xs, merged, w_o16, l, gt1, g_ffn[l], sc2, sh2, rs)
        act = _ffn_up_call(h2, w_gate16, w_up16, l, rs, 512)
        xs, nxt = _ffn_down_call(xs, act, w_down16, l, gt2, g_next, mod_s[ln, :, :, 1], mod_s[ln, :, :, 0],
                                 rs, tk_ffn, last)
        hs, ys = (None, nxt) if last else (nxt, None)
        outs["ks"].append(k32[0, :db].reshape(db, ts, N_HEADS, HEAD_DIM))
        outs["vs"].append(v32[0, :db].reshape(db, ts, N_HEADS, HEAD_DIM))
        outs["kis"].append(kw[0, :db, :D_IDX].reshape(db, ts, D_IDX))
        outs["cs"].append(jnp.stack([prev[:db, 1], u[:db]], axis=1))

    page_shape = (depth, bp, tp // PAGE_SIZE, PAGE_SIZE, N_HEADS, HEAD_DIM)
    return (yp, ys[0, :db].reshape(db, ts, d),
            kp_all.reshape(page_shape), vp_all.reshape(page_shape), jnp.stack(outs["kip"]), jnp.stack(outs["cp"]),
            jnp.stack(outs["ks"]), jnp.stack(outs["vs"]), jnp.stack(outs["kis"]), jnp.stack(outs["cs"]))
```

```python
import functools

import jax
import jax.numpy as jnp
import numpy as np
from jax import lax
from jax.experimental import pallas as pl
from jax.experimental.pallas import tpu as pltpu

F32 = jnp.float32
BF16 = jnp.bfloat16
I32 = jnp.int32
I16 = jnp.int16

N_HEADS = 8
HEAD_DIM = 128
N_IDX_HEADS = 16
D_IDX = 64
CONV_W = 3
TOPK_MAX = 256
PAGE_SIZE = 128
ROPE_THETA = 10000.0
EPS = 1e-6
INDEX_SCALE = (D_IDX ** -0.5) * (N_IDX_HEADS ** -0.5)
ATT_SCALE = HEAD_DIM ** -0.5
EXP2_SCALE = ATT_SCALE * 1.4426950408889634

LANES = 128
SUBLANES = 8
VMEM_LIMIT = 56 * 1024 * 1024
INT_MIN = -2 ** 31
HALF16 = 2 ** 15
M_FLOOR = -1e30
ATT_BLOCK = 256


def _params(sem):
    return pltpu.CompilerParams(dimension_semantics=sem, vmem_limit_bytes=VMEM_LIMIT)


def _dot(a, b):
    return jnp.dot(a, b, preferred_element_type=F32)


def _dot_nt(a, b):
    return lax.dot_general(a, b, (((1,), (1,)), ((), ())), preferred_element_type=F32)


def _rmsnorm(x):
    return x * lax.rsqrt(jnp.mean(x * x, axis=-1, keepdims=True) + EPS)


def _sort_key(s):
    s = jnp.where(s == 0.0, 0.0, s)
    b = pltpu.bitcast(s, I32)
    return jnp.where(b < 0, b ^ 0x7FFFFFFF, b)


def _w_spec(k, tn, layer, col0=0, col_axis=None, row_axis=None):
    def index(*ids):
        return (layer, 0 if row_axis is None else ids[row_axis], col0 + (0 if col_axis is None else ids[col_axis]))
    return pl.BlockSpec((None, k, tn), index)


def _mod_kernel(c_ref, w_ref, b_ref, o_ref):
    c = c_ref[...]
    a = (c * jax.nn.sigmoid(c)).astype(BF16)
    o_ref[...] = _dot(a, w_ref[...].astype(BF16)) + b_ref[...]


def _mod_call(c_all, w_mod, b_mod):
    depth, d, n = w_mod.shape
    rows = c_all.shape[0]
    tn = 1024
    return pl.pallas_call(
        _mod_kernel,
        grid=(depth, n // tn),
        in_specs=[pl.BlockSpec((rows, d), lambda l, j: (0, 0)),
                  pl.BlockSpec((None, d, tn), lambda l, j: (l, 0, j)),
                  pl.BlockSpec((None, 1, tn), lambda l, j: (l, 0, j))],
        out_specs=pl.BlockSpec((None, rows, tn), lambda l, j: (l, 0, j)),
        out_shape=jax.ShapeDtypeStruct((depth, rows, n), F32),
        compiler_params=_params(("arbitrary", "arbitrary")),
        name="mod",
    )(c_all, w_mod, b_mod.reshape(depth, 1, n))


def _mod_spec(mod, tm, row_axis):
    rm, d = mod.shape[1:]
    if rm == 1:
        return pl.BlockSpec((None, 1, d), lambda *ids: (ids[0], 0, 0))
    return pl.BlockSpec((None, tm, d), lambda *ids: (ids[0], ids[row_axis], 0))


def _norm_mod_kernel(x_ref, g_ref, sc_ref, sh_ref, o_ref):
    y = _rmsnorm(x_ref[...]) * g_ref[...]
    o_ref[...] = (y * (1.0 + sc_ref[...]) + sh_ref[...]).astype(o_ref.dtype)


def _norm_mod_call(x, g, sc, sh, tm):
    gn, r, d = x.shape
    return pl.pallas_call(
        _norm_mod_kernel,
        grid=(gn, r // tm),
        in_specs=[pl.BlockSpec((None, tm, d), lambda b, i: (b, i, 0)),
                  pl.BlockSpec((1, d), lambda b, i: (0, 0)),
                  _mod_spec(sc, tm, 1), _mod_spec(sh, tm, 1)],
        out_specs=pl.BlockSpec((None, tm, d), lambda b, i: (b, i, 0)),
        out_shape=jax.ShapeDtypeStruct((gn, r, d), BF16),
        compiler_params=_params(("arbitrary", "arbitrary")),
        name="norm_mod",
    )(x, g.reshape(1, d), sc, sh)


def _conv_seq_kernel(h_ref, wb_ref, wc_ref, wv_ref, cw_ref, prev_ref, a_ref, st_ref, buf, *, tm):
    i = pl.program_id(2)
    h = h_ref[...]
    bg = _dot(h, wb_ref[...])
    u = _dot(h, wc_ref[...]) * _dot(h, wv_ref[...])

    @pl.when(i == 0)
    def _():
        buf[6:8, :] = prev_ref[...]

    buf[8:8 + tm, :] = u
    cw = cw_ref[...]
    y = cw[0:1] * buf[6:6 + tm, :] + cw[1:2] * buf[7:7 + tm, :] + cw[2:3] * u
    a_ref[...] = (bg * y).astype(BF16)
    last2 = buf[6 + tm:8 + tm, :]
    buf[6:8, :] = last2
    st_ref[...] = last2


def _conv_seq_call(h, w_in, layer, cw, prev, tm, tn):
    gn, r, d = h.shape
    n = cw.shape[1]
    nj = n // tn
    return pl.pallas_call(
        functools.partial(_conv_seq_kernel, tm=tm),
        grid=(gn, nj, r // tm),
        in_specs=[pl.BlockSpec((None, tm, d), lambda b, j, i: (b, i, 0)),
                  _w_spec(d, tn, layer, 0, 1), _w_spec(d, tn, layer, nj, 1), _w_spec(d, tn, layer, 2 * nj, 1),
                  pl.BlockSpec((CONV_W, tn), lambda b, j, i: (0, j)),
                  pl.BlockSpec((None, CONV_W - 1, tn), lambda b, j, i: (b, 0, j))],
        out_specs=[pl.BlockSpec((None, tm, tn), lambda b, j, i: (b, i, j)),
                   pl.BlockSpec((None, CONV_W - 1, tn), lambda b, j, i: (b, 0, j))],
        out_shape=[jax.ShapeDtypeStruct((gn, r, n), BF16),
                   jax.ShapeDtypeStruct((gn, CONV_W - 1, n), F32)],
        scratch_shapes=[pltpu.VMEM((8 + tm, tn), F32)],
        compiler_params=_params(("arbitrary", "arbitrary", "arbitrary")),
        name="conv_seq",
    )(h, w_in, w_in, w_in, cw, prev)


def _conv_step_kernel(h_ref, wb_ref, wc_ref, wv_ref, cw_ref, p0_ref, p1_ref, a_ref, u_ref):
    h = h_ref[...]
    bg = _dot(h, wb_ref[...])
    u = _dot(h, wc_ref[...]) * _dot(h, wv_ref[...])
    cw = cw_ref[...]
    y = cw[0:1] * p0_ref[...] + cw[1:2] * p1_ref[...] + cw[2:3] * u
    a_ref[...] = (bg * y).astype(BF16)
    u_ref[...] = u


def _conv_step_call(h, w_in, layer, cw, p0, p1, tn):
    r, d = h.shape
    n = cw.shape[1]
    nj = n // tn
    row_tile = pl.BlockSpec((r, tn), lambda j: (0, j))
    return pl.pallas_call(
        _conv_step_kernel,
        grid=(nj,),
        in_specs=[pl.BlockSpec((r, d), lambda j: (0, 0)),
                  _w_spec(d, tn, layer, 0, 0), _w_spec(d, tn, layer, nj, 0), _w_spec(d, tn, layer, 2 * nj, 0),
                  pl.BlockSpec((CONV_W, tn), lambda j: (0, j)),
                  row_tile, row_tile],
        out_specs=[row_tile, row_tile],
        out_shape=[jax.ShapeDtypeStruct((r, n), BF16), jax.ShapeDtypeStruct((r, n), F32)],
        compiler_params=_params(("arbitrary",)),
        name="conv_step",
    )(h, w_in, w_in, w_in, cw, p0, p1)


def _rope_group(x, cos, sin, half):
    if 2 * half == LANES:
        rot = pltpu.roll(x, half, 1)
    else:
        lane = lax.broadcasted_iota(I32, x.shape, 1)
        rot = jnp.where(lane % (2 * half) < half, pltpu.roll(x, LANES - half, 1), pltpu.roll(x, half, 1))
    return x * cos + rot * sin


def _proj_kernel(*refs, half, outs, tck, aliased):
    h_ref, w_ref, cos_ref, sin_ref = refs[:4]
    out_refs = list(refs[4 + aliased:])
    acc = _dot(h_ref[...], w_ref[...])
    cos, sin = cos_ref[...], sin_ref[...]
    groups = []
    for c in range(acc.shape[1] // LANES):
        x = acc[:, c * LANES:(c + 1) * LANES]
        groups.append(_rope_group(x, cos, sin, half) if half else x)
    for kind in outs:
        o = out_refs.pop(0)
        for c, x in enumerate(groups):
            if kind == "f32":
                o[:, c * LANES:(c + 1) * LANES] = x
            elif kind == "bf16":
                o[:, c * LANES:(c + 1) * LANES] = x.astype(BF16)
            elif kind == "heads":
                o[:, c, :] = x
            elif kind == "idx_heads":
                xb = x.astype(BF16)
                o[2 * c] = xb[:, :D_IDX]
                o[2 * c + 1] = xb[:, D_IDX:]
            elif kind == "chunk_t":
                for k in range(x.shape[0] // tck):
                    o[k, c * LANES:(c + 1) * LANES, :] = x[k * tck:(k + 1) * tck, :].T.astype(BF16)


def _proj_call(h, w_in, layer, col0, n, cos, sin, tm, *, half, outs, tck=None, cache=None):
    gn, r, d = h.shape
    depth = None if cache is None else cache.shape[0]
    heads = n // LANES
    out_specs, out_shape = [], []
    for kind in outs:
        if kind in ("f32", "bf16"):
            out_specs.append(pl.BlockSpec((None, tm, n), lambda b, i: (b, i, 0)))
            out_shape.append(jax.ShapeDtypeStruct((gn, r, n), F32 if kind == "f32" else BF16))
        elif kind == "heads":
            out_specs.append(pl.BlockSpec((None, None, tm, heads, LANES), lambda b, i: (layer, b, i, 0, 0)))
            out_shape.append(jax.ShapeDtypeStruct((depth, gn, r, heads, LANES), F32))
        elif kind == "idx_heads":
            out_specs.append(pl.BlockSpec((None, n // D_IDX, tm, D_IDX), lambda b, i: (b, 0, i, 0)))
            out_shape.append(jax.ShapeDtypeStruct((gn, n // D_IDX, r, D_IDX), BF16))
        elif kind == "chunk_t":
            out_specs.append(pl.BlockSpec((None, tm // tck, n, tck), lambda b, i: (b, i, 0, 0)))
            out_shape.append(jax.ShapeDtypeStruct((gn, r // tck, n, tck), BF16))
    in_specs = [pl.BlockSpec((None, tm, d), lambda b, i: (b, i, 0)),
                _w_spec(d, n, layer, col0),
                pl.BlockSpec((tm, LANES), lambda b, i: (i, 0)),
                pl.BlockSpec((tm, LANES), lambda b, i: (i, 0))]
    args = [h, w_in, cos, sin]
    aliases = {}
    if cache is not None:
        in_specs.append(pl.BlockSpec(memory_space=pl.ANY))
        args.append(cache)
        aliases = {4: outs.index("heads")}
    return pl.pallas_call(
        functools.partial(_proj_kernel, half=half, outs=outs, tck=tck, aliased=cache is not None),
        grid=(gn, r // tm),
        in_specs=in_specs, out_specs=out_specs, out_shape=out_shape,
        input_output_aliases=aliases,
        compiler_params=_params(("arbitrary", "arbitrary")),
        name="proj",
    )(*args)


def _kidx_kernel(h_ref, w_ref, cos_ref, sin_ref, kw_ref, kb_ref, wt_ref):
    acc = _dot(h_ref[...], w_ref[...])
    roped = _rope_group(acc, cos_ref[...], sin_ref[...], D_IDX // 2)
    lane = lax.broadcasted_iota(I32, acc.shape, 1)
    kw = jnp.where(lane < D_IDX, roped, jnp.where(lane < D_IDX + N_IDX_HEADS, acc, 0.0))
    kw_ref[...] = kw
    kb_ref[...] = kw[:, :D_IDX].astype(BF16)
    wt_ref[...] = kw.T[D_IDX:D_IDX + N_IDX_HEADS, :]


def _kidx_call(h, w_in, layer, col0, cos, sin, tm):
    gn, r, d = h.shape
    return pl.pallas_call(
        _kidx_kernel,
        grid=(gn, r // tm),
        in_specs=[pl.BlockSpec((None, tm, d), lambda b, i: (b, i, 0)),
                  _w_spec(d, LANES, layer, col0),
                  pl.BlockSpec((tm, LANES), lambda b, i: (i, 0)),
                  pl.BlockSpec((tm, LANES), lambda b, i: (i, 0))],
        out_specs=[pl.BlockSpec((None, tm, LANES), lambda b, i: (b, i, 0)),
                   pl.BlockSpec((None, tm, D_IDX), lambda b, i: (b, i, 0)),
                   pl.BlockSpec((None, N_IDX_HEADS, tm), lambda b, i: (b, 0, i))],
        out_shape=[jax.ShapeDtypeStruct((gn, r, LANES), F32), jax.ShapeDtypeStruct((gn, r, D_IDX), BF16),
                   jax.ShapeDtypeStruct((gn, N_IDX_HEADS, r), F32)],
        compiler_params=_params(("arbitrary", "arbitrary")),
        name="kidx",
    )(h, w_in, cos, sin)


def _strict_tri(n, dtype, lower):
    row = lax.broadcasted_iota(I32, (n, n), 0)
    col = lax.broadcasted_iota(I32, (n, n), 1)
    return jnp.where(col < row if lower else row < col, 1.0, 0.0).astype(dtype)


def _dsa_kernel(q_ref, qi_ref, wi_ref, k_ref, vt_ref, ki_ref, o_ref,
                key_scr, hi_scr, lo_scr, m_scr, l_scr, acc_scr, *, topk, ck, qb):
    i = pl.program_id(1)
    nch = ((i + 1) * qb + ck - 1) // ck
    qpos = i * qb + lax.broadcasted_iota(I32, (LANES, qb), 1)
    krow = lax.broadcasted_iota(I32, (LANES, qb), 0)
    w = wi_ref[...]

    def index_chunk(c, _):
        base = pl.multiple_of(c * ck, ck)
        for r in range(ck // LANES):
            rows = slice(r * LANES, (r + 1) * LANES)
            kc = ki_ref[pl.ds(base + r * LANES, LANES), :]
            acc = jnp.zeros((LANES, qb), F32)
            for h in range(N_IDX_HEADS):
                acc = acc + jnp.maximum(_dot_nt(kc, qi_ref[h]), 0.0) * w[h:h + 1, :]
            kpos = base + r * LANES + krow
            key = jnp.where(kpos <= qpos, _sort_key(acc * INDEX_SCALE), INT_MIN)
            key_scr[c, rows, :] = key
            hi_scr[c, rows, :] = lax.shift_right_arithmetic(key, 16).astype(I16)
            lo_scr[c, rows, :] = ((key & 0xFFFF) - HALF16).astype(I16)
        return 0

    lax.fori_loop(0, nch, index_chunk, 0)

    ge = lambda a, b: a >= b
    gt = lambda a, b: a > b
    pack = 2 * SUBLANES

    def count16(scr, pred, thr):
        thr = thr.astype(I16)

        def body(c, part):
            hit = jnp.where(pred(scr[c], thr), jnp.ones((), BF16), jnp.zeros((), BF16))
            terms = [hit[k * pack:(k + 1) * pack] for k in range(ck // pack)]
            while len(terms) > 1:
                terms = [a + b for a, b in zip(terms[0::2], terms[1::2])]
            return part + terms[0]

        part = lax.fori_loop(0, nch, body, jnp.zeros((pack, qb), BF16))
        return jnp.sum(part.astype(F32), axis=0, keepdims=True).astype(I32)

    def search16(scr, need):
        def step(it, t):
            cand = t + jnp.left_shift(jnp.int32(1), 15 - it)
            return jnp.where(count16(scr, ge, cand) >= need, cand, t)
        return lax.fori_loop(0, 16, step, jnp.full((1, qb), -HALF16, I32))

    t_hi = search16(hi_scr, topk)
    need_lo = topk - count16(hi_scr, gt, t_hi)
    t_hi16 = t_hi.astype(I16)

    def keep_band(c, _):
        lo_scr[c] = jnp.where(hi_scr[c] == t_hi16, lo_scr[c], jnp.full((), -HALF16, I16))
        return 0

    lax.fori_loop(0, nch, keep_band, 0)
    t_lo = search16(lo_scr, need_lo)
    t = t_hi * (2 * HALF16) + (t_lo + HALF16)
    t = jnp.maximum(t, INT_MIN + 1)

    def count(pred, thr):
        def body(c, cnt):
            hit = jnp.where(pred(key_scr[c], thr), 1, 0)
            return cnt + jnp.sum(hit.reshape(ck // SUBLANES, SUBLANES, qb), axis=0)
        cnt = lax.fori_loop(0, nch, body, jnp.zeros((SUBLANES, qb), I32))
        return jnp.sum(cnt, axis=0, keepdims=True)

    c_ge = count(ge, t)

    @pl.when(jnp.max(c_ge) > topk)
    def _():
        need = (topk - count(gt, t)).astype(F32)
        lower = _strict_tri(ck, BF16, lower=True)

        def body(c, carry):
            kc = key_scr[c]
            eq = kc == t
            eqf = jnp.where(eq, 1.0, 0.0)
            before = _dot(lower, eqf.astype(BF16)) + carry
            key_scr[c] = jnp.where(eq & (before >= need), INT_MIN, kc)
            return carry + jnp.sum(eqf, axis=0, keepdims=True)

        lax.fori_loop(0, nch, body, jnp.zeros((1, qb), F32))

    m_scr[...] = jnp.full(m_scr.shape, M_FLOOR, F32)
    l_scr[...] = jnp.zeros(l_scr.shape, F32)
    acc_scr[...] = jnp.zeros(acc_scr.shape, F32)

    def attend_chunk(c, _):
        base = pl.multiple_of(c * ck, ck)
        sel = key_scr[c] >= t
        heads = [slice(h * HEAD_DIM, (h + 1) * HEAD_DIM) for h in range(N_HEADS)]
        raw = [_dot_nt(k_ref[pl.ds(base, ck), cols], q_ref[:, cols]) for cols in heads]
        probs = []
        for h in range(N_HEADS):
            s = jnp.where(sel, raw[h], -jnp.inf)
            m_old = m_scr[h]
            m_new = jnp.maximum(m_old, jnp.max(s, axis=0, keepdims=True))
            alpha = jnp.exp2((m_old - m_new) * EXP2_SCALE)
            p = jnp.exp2((s - m_new) * EXP2_SCALE)
            l_scr[h] = alpha * l_scr[h] + jnp.sum(p, axis=0, keepdims=True)
            m_scr[h] = m_new
            probs.append((alpha, p.astype(BF16)))
        for h in range(N_HEADS):
            alpha, p = probs[h]
            acc_scr[h] = alpha * acc_scr[h] + _dot(vt_ref[c, heads[h], :], p)
        return 0

    lax.fori_loop(0, nch, attend_chunk, 0)
    for h in range(N_HEADS):
        o_ref[:, h * HEAD_DIM:(h + 1) * HEAD_DIM] = (acc_scr[h] / l_scr[h]).T.astype(o_ref.dtype)


def _dsa_call(q, qi_hm, wi_t, k, v_t, ki, topk, blk):
    bn, t, a = q.shape
    assert t <= 256 * 2 * SUBLANES, "the threshold search counts in bf16: at most 256 keys per packed-row entry"
    full = lambda shape: pl.BlockSpec((None,) + shape, lambda b, i: (b,) + (0,) * len(shape))
    return pl.pallas_call(
        functools.partial(_dsa_kernel, topk=topk, ck=blk, qb=blk),
        grid=(bn, t // blk),
        in_specs=[pl.BlockSpec((None, blk, a), lambda b, i: (b, i, 0)),
                  pl.BlockSpec((None, N_IDX_HEADS, blk, D_IDX), lambda b, i: (b, 0, i, 0)),
                  pl.BlockSpec((None, N_IDX_HEADS, blk), lambda b, i: (b, 0, i)),
                  full((t, a)), full((t // blk, a, blk)), full((t, D_IDX))],
        out_specs=pl.BlockSpec((None, blk, a), lambda b, i: (b, i, 0)),
        out_shape=jax.ShapeDtypeStruct((bn, t, a), BF16),
        scratch_shapes=[pltpu.VMEM((t // blk, blk, blk), I32),
                        pltpu.VMEM((t // blk, blk, blk), I16),
                        pltpu.VMEM((t // blk, blk, blk), I16),
                        pltpu.VMEM((N_HEADS, 1, blk), F32),
                        pltpu.VMEM((N_HEADS, 1, blk), F32),
                        pltpu.VMEM((N_HEADS, HEAD_DIM, blk), F32)],
        compiler_params=_params(("arbitrary", "arbitrary")),
        name="dsa_prompt",
    )(q, qi_hm, wi_t, k, v_t, ki)


def _pool_scores_kernel(qi_ref, w_ref, ck_ref, o_ref, *, nb, group):
    pages, _, page = ck_ref.shape
    qi, w = qi_ref[...], w_ref[...]
    for p in range(0, pages, group):
        kc = jnp.concatenate([ck_ref[p + g] for g in range(group)], axis=1).astype(BF16)
        d = jnp.maximum(_dot(qi, kc), 0.0) * w
        s = d[0:nb]
        for h in range(1, N_IDX_HEADS):
            s = s + d[h * nb:(h + 1) * nb]
        o_ref[:, p * page:(p + group) * page] = s * INDEX_SCALE


def _pool_scores_call(qi_hb, w_hb, kidx_t, layer, nb, pages_per_step):
    depth, n_pool, di, page = kidx_t.shape
    group = 2 if pages_per_step % 2 == 0 else 1
    return pl.pallas_call(
        functools.partial(_pool_scores_kernel, nb=nb, group=group),
        grid=(n_pool // pages_per_step,),
        in_specs=[pl.BlockSpec(qi_hb.shape, lambda j: (0, 0)),
                  pl.BlockSpec(w_hb.shape, lambda j: (0, 0)),
                  pl.BlockSpec((None, pages_per_step, di, page), lambda j: (layer, j, 0, 0))],
        out_specs=pl.BlockSpec((nb, pages_per_step * page), lambda j: (0, j)),
        out_shape=jax.ShapeDtypeStruct((nb, n_pool * page), F32),
        compiler_params=_params(("arbitrary",)),
        name="pool_scores",
    )(qi_hb, w_hb, kidx_t)


def _select_kernel(pt_ref, sc_ref, ptv_ref, qi_ref, w_ref, kn_ref, row_ref, s_scr, *, topk, n_pages, nb):
    def gather_page(p, _):
        for b in range(nb):
            s_scr[b, pl.ds(p, 1), :] = sc_ref[b, pl.ds(pt_ref[b, p], 1), :]
        return 0

    lax.fori_loop(0, n_pages, gather_page, 0)
    key = _sort_key(s_scr[...])

    def total(x):
        return jnp.sum(jnp.sum(x, axis=1, keepdims=True), axis=2, keepdims=True)

    d = jnp.maximum(_dot_nt(qi_ref[...], kn_ref[...]), 0.0) * w_ref[...]
    row = lax.broadcasted_iota(I32, d.shape, 0)
    col = lax.broadcasted_iota(I32, d.shape, 1)
    s_new = jnp.stack([jnp.sum(jnp.where((row % nb == b) & (col == b), d, 0.0), keepdims=True) for b in range(nb)])
    key_new = _sort_key(s_new * INDEX_SCALE)

    def search(it, t):
        cand = t + jnp.left_shift(jnp.int32(1), 31 - it)
        n_ge = total(jnp.where(key >= cand, 1, 0)) + jnp.where(key_new >= cand, 1, 0)
        return jnp.where(n_ge >= topk, cand, t)

    t = lax.fori_loop(0, 32, search, jnp.full((nb, 1, 1), INT_MIN, I32))
    t = jnp.maximum(t, INT_MIN + 1)

    upper = _strict_tri(PAGE_SIZE, BF16, lower=False)
    lower = _strict_tri(n_pages, BF16, lower=True)
    ones = jnp.ones((SUBLANES, PAGE_SIZE), BF16)
    incl = jnp.where(lax.broadcasted_iota(I32, (n_pages, n_pages), 0) <= lax.broadcasted_iota(I32, (n_pages, n_pages), 1),
                     1.0, 0.0).astype(BF16)
    slot = lax.broadcasted_iota(I32, (topk, PAGE_SIZE), 0).astype(F32)
    lane = lax.broadcasted_iota(I32, (topk, PAGE_SIZE), 1)
    page_lane = lax.broadcasted_iota(I32, (topk, n_pages), 1)

    def ranks(f16):
        per_page = jnp.broadcast_to(jnp.sum(f16.astype(F32), axis=1, keepdims=True), f16.shape)
        return _dot(f16, upper), _dot(lower, per_page.astype(BF16))

    for b in range(nb):
        kb, tb, knb = key[b], t[b], key_new[b]
        n_gt = jnp.sum(jnp.where(kb > tb, 1.0, 0.0), keepdims=True) + jnp.where(knb > tb, 1.0, 0.0)
        need = topk - n_gt
        eq = kb == tb
        eq16 = jnp.where(eq, 1.0, 0.0).astype(BF16)
        eq_in, eq_before = ranks(eq16)
        sel = (kb > tb) | (eq & (eq_in + eq_before < need))
        sel_new = (knb > tb) | ((knb == tb) & (jnp.sum(eq16.astype(F32), keepdims=True) < need))

        sel16 = jnp.where(sel, 1.0, 0.0).astype(BF16)
        in_page, before_page = ranks(sel16)
        n_sel = jnp.sum(sel16.astype(F32), keepdims=True)
        cum = _dot(_dot_nt(ones, sel16).astype(BF16), incl)[0:1, :]
        page_r = jnp.sum(jnp.where(cum <= slot[:, 0:1], 1.0, 0.0), axis=1, keepdims=True).astype(I32)
        hit = page_lane == page_r
        onehot = jnp.where(hit, 1.0, 0.0).astype(BF16)
        got = _dot(onehot, jnp.where(sel, in_page, -1.0).astype(BF16))
        within = slot - _dot(onehot, before_page.astype(BF16))
        off = jnp.sum(jnp.where(got == within, lane, 0), axis=1, keepdims=True)
        phys = jnp.sum(jnp.where(hit, ptv_ref[b:b + 1, :], 0), axis=1, keepdims=True)
        rows = jnp.where(slot[:, 0:1] < n_sel, phys * PAGE_SIZE + off, 0)
        row_ref[b] = rows - jnp.where(sel_new & (slot[:, 0:1] == n_sel), 1, 0)


def _select_call(page_table, scores, qi_hb, w_hb, k_new, topk):
    nb, n_pages = page_table.shape
    whole = lambda a: pl.BlockSpec(a.shape, lambda i, pt: (0,) * a.ndim)
    return pl.pallas_call(
        functools.partial(_select_kernel, topk=topk, n_pages=n_pages, nb=nb),
        grid_spec=pltpu.PrefetchScalarGridSpec(
            num_scalar_prefetch=1, grid=(1,),
            in_specs=[whole(scores), whole(page_table), whole(qi_hb), whole(w_hb), whole(k_new)],
            out_specs=pl.BlockSpec((nb, topk, 1), lambda i, pt: (0, 0, 0)),
            scratch_shapes=[pltpu.VMEM((nb, n_pages, PAGE_SIZE), F32)]),
        out_shape=jax.ShapeDtypeStruct((nb, topk, 1), I32),
        compiler_params=_params(("arbitrary",)),
        name="select",
    )(page_table, scores, page_table, qi_hb, w_hb, k_new)


def _gather_attend_kernel(row_ref, q_ref, kn_ref, vn_ref, ck_hbm, cv_hbm, o_ref, kbuf, vbuf, sem, *, layer, topk):
    b = pl.program_id(0)

    def copies(r):
        src = jnp.maximum(row_ref[b, r], 0)
        return (pltpu.make_async_copy(ck_hbm.at[layer, src], kbuf.at[r], sem.at[0]),
                pltpu.make_async_copy(cv_hbm.at[layer, src], vbuf.at[r], sem.at[1]))

    def start(r, _):
        for cp in copies(r):
            cp.start()
        return 0

    def wait(r, _):
        for cp in copies(r):
            cp.wait()
        return 0

    lax.fori_loop(0, topk, start, 0)
    lax.fori_loop(0, topk, wait, 0)

    @pl.when(row_ref[b, topk - 1] < 0)
    def _():
        kbuf[topk - 1] = kn_ref[...]
        vbuf[topk - 1] = vn_ref[...]

    q = q_ref[...]
    s = jnp.sum(kbuf[...] * q[None], axis=-1, keepdims=True) * ATT_SCALE
    p = jnp.exp(s - jnp.max(s, axis=0, keepdims=True))
    p = p / jnp.sum(p, axis=0, keepdims=True)
    o_ref[...] = jnp.sum(p * vbuf[...], axis=0)


def _gather_attend_call(rows, q, k_new, v_new, cache_k, cache_v, layer, topk):
    nb = rows.shape[0]
    any_spec = pl.BlockSpec(memory_space=pl.ANY)
    per_sample = pl.BlockSpec((None, N_HEADS, HEAD_DIM), lambda b, rows: (b, 0, 0))
    return pl.pallas_call(
        functools.partial(_gather_attend_kernel, layer=layer, topk=topk),
        grid_spec=pltpu.PrefetchScalarGridSpec(
            num_scalar_prefetch=1, grid=(nb,),
            in_specs=[per_sample, per_sample, per_sample, any_spec, any_spec],
            out_specs=per_sample,
            scratch_shapes=[pltpu.VMEM((topk, N_HEADS, HEAD_DIM), F32),
                            pltpu.VMEM((topk, N_HEADS, HEAD_DIM), F32),
                            pltpu.SemaphoreType.DMA((2,))]),
        out_shape=jax.ShapeDtypeStruct((nb, N_HEADS, HEAD_DIM), F32),
        compiler_params=_params(("arbitrary",)),
        name="gather_attend",
    )(rows, q, k_new, v_new, cache_k, cache_v)


def _merge_kernel(h_ref, a_ref, t_ref, wga_ref, wgb_ref, wpa_ref, wpb_ref, o_ref):
    h = h_ref[...]
    ya = _dot(a_ref[...], wpa_ref[...])
    yb = _dot(t_ref[...], wpb_ref[...])
    ga = jax.nn.sigmoid(_dot(h, wga_ref[...]))
    gb = jax.nn.sigmoid(_dot(h, wgb_ref[...]))
    o_ref[...] = (ga * ya + gb * yb).astype(BF16)


def _merge_call(h, a_in, att, w_gates, w_pa, w_pb, layer, tm, tn):
    gn, r, d = h.shape
    a = a_in.shape[2]
    rows = lambda w: pl.BlockSpec((None, tm, w), lambda b, j, i: (b, i, 0))
    return pl.pallas_call(
        _merge_kernel,
        grid=(gn, d // tn, r // tm),
        in_specs=[rows(d), rows(a), rows(a),
                  _w_spec(d, tn, layer, 0, 1), _w_spec(d, tn, layer, d // tn, 1),
                  _w_spec(a, tn, layer, 0, 1), _w_spec(a, tn, layer, 0, 1)],
        out_specs=pl.BlockSpec((None, tm, tn), lambda b, j, i: (b, i, j)),
        out_shape=jax.ShapeDtypeStruct((gn, r, d), BF16),
        compiler_params=_params(("arbitrary", "arbitrary", "arbitrary")),
        name="merge",
    )(h, a_in, att, w_gates, w_gates, w_pa, w_pb)


def _resid_norm_kernel(x_ref, m_ref, w_ref, gt_ref, g_ref, sc_ref, sh_ref, x_out, h_out):
    x = x_ref[...] + gt_ref[...] * _dot(m_ref[...], w_ref[...])
    x_out[...] = x
    y = _rmsnorm(x) * g_ref[...]
    h_out[...] = (y * (1.0 + sc_ref[...]) + sh_ref[...]).astype(BF16)


def _resid_norm_call(x, m, w_o, layer, gt, g, sc, sh, tm):
    gn, r, d = x.shape
    row = lambda: pl.BlockSpec((None, tm, d), lambda b, i: (b, i, 0))
    return pl.pallas_call(
        _resid_norm_kernel,
        grid=(gn, r // tm),
        in_specs=[row(), row(), _w_spec(d, d, layer),
                  _mod_spec(gt, tm, 1), pl.BlockSpec((1, d), lambda b, i: (0, 0)),
                  _mod_spec(sc, tm, 1), _mod_spec(sh, tm, 1)],
        out_specs=[row(), row()],
        out_shape=[jax.ShapeDtypeStruct((gn, r, d), F32), jax.ShapeDtypeStruct((gn, r, d), BF16)],
        compiler_params=_params(("arbitrary", "arbitrary")),
        name="resid_norm",
    )(x, m, w_o, gt, g.reshape(1, d), sc, sh)


def _ffn_up_kernel(h_ref, wg_ref, wu_ref, o_ref):
    h = h_ref[...]
    g = _dot(h, wg_ref[...])
    o_ref[...] = (g * jax.nn.sigmoid(g) * _dot(h, wu_ref[...])).astype(BF16)


def _ffn_up_call(h, w_gate, w_up, layer, tm, tn):
    gn, r, d = h.shape
    f = w_gate.shape[2]
    return pl.pallas_call(
        _ffn_up_kernel,
        grid=(gn, f // tn, r // tm),
        in_specs=[pl.BlockSpec((None, tm, d), lambda b, j, i: (b, i, 0)),
                  _w_spec(d, tn, layer, 0, 1), _w_spec(d, tn, layer, 0, 1)],
        out_specs=pl.BlockSpec((None, tm, tn), lambda b, j, i: (b, i, j)),
        out_shape=jax.ShapeDtypeStruct((gn, r, f), BF16),
        compiler_params=_params(("arbitrary", "arbitrary", "arbitrary")),
        name="ffn_up",
    )(h, w_gate, w_up)


def _ffn_down_kernel(x_ref, a_ref, w_ref, gt_ref, g_ref, sc_ref, sh_ref, x_out, n_out, acc, *, final):
    k = pl.program_id(2)

    @pl.when(k == 0)
    def _():
        acc[...] = jnp.zeros(acc.shape, F32)

    acc[...] += _dot(a_ref[...], w_ref[...])

    @pl.when(k == pl.num_programs(2) - 1)
    def _():
        x = x_ref[...] + gt_ref[...] * acc[...]
        x_out[...] = x
        y = _rmsnorm(x) * g_ref[...]
        if not final:
            y = y * (1.0 + sc_ref[...]) + sh_ref[...]
        n_out[...] = y.astype(n_out.dtype)


def _ffn_down_call(x, act, w_down, layer, gt, g, sc, sh, tm, tk, final):
    gn, r, d = x.shape
    f = act.shape[2]
    row = lambda: pl.BlockSpec((None, tm, d), lambda b, i, k: (b, i, 0))
    return pl.pallas_call(
        functools.partial(_ffn_down_kernel, final=final),
        grid=(gn, r // tm, f // tk),
        in_specs=[row(), pl.BlockSpec((None, tm, tk), lambda b, i, k: (b, i, k)),
                  _w_spec(tk, d, layer, row_axis=2),
                  _mod_spec(gt, tm, 1), pl.BlockSpec((1, d), lambda b, i, k: (0, 0)),
                  _mod_spec(sc, tm, 1), _mod_spec(sh, tm, 1)],
        out_specs=[row(), row()],
        out_shape=[jax.ShapeDtypeStruct((gn, r, d), F32),
                   jax.ShapeDtypeStruct((gn, r, d), F32 if final else BF16)],
        scratch_shapes=[pltpu.VMEM((tm, d), F32)],
        compiler_params=_params(("arbitrary", "arbitrary", "arbitrary")),
        name="ffn_down",
    )(x, act, w_down, gt, g.reshape(1, d), sc, sh)


def _rope_tables(pos, d):
    inv_freq = ROPE_THETA ** (-np.arange(0, d, 2, dtype=np.float64) / d)
    ang = np.asarray(pos, np.float64)[:, None] * inv_freq[None, :]
    cos, sin = np.cos(ang), np.sin(ang)
    reps = LANES // d
    return (jnp.asarray(np.tile(np.concatenate([cos, cos], axis=1), (1, reps)), F32),
            jnp.asarray(np.tile(np.concatenate([-sin, sin], axis=1), (1, reps)), F32))


def _pages_per_step(n_pool):
    return max(p for p in range(1, 33) if n_pool % p == 0)


def kernel(x_prompt, x_sample, c_prompt, c_sample, cache_k, cache_v, cache_kidx, state_conv, page_table,
           w_mod, b_mod, g_mix, g_ffn, w_in, conv_w, w_pa, w_pb, w_o, w_gate, w_up, w_down, g_final):
    bp, tp, d = x_prompt.shape
    db, ts, _ = x_sample.shape
    assert ts == 1, "the sample path handles one new token per sequence"
    depth = w_mod.shape[0]
    n_pages = page_table.shape[1]
    past = n_pages * PAGE_SIZE
    att_dim = N_HEADS * HEAD_DIM
    d_conv = d // 2
    idx_q = N_IDX_HEADS * D_IDX
    topk_p = min(TOPK_MAX, tp // 4)
    topk_s = min(TOPK_MAX, (past + ts) // 4)
    rs = 16
    blk = ATT_BLOCK if tp % ATT_BLOCK == 0 else LANES
    assert db <= rs and tp % blk == 0 and d_conv == att_dim == idx_q

    cb_q, cb_k, cb_v, cb_qi = 3, 4, 5, 6
    o_ki = 7 * att_dim
    o_ga = o_ki + D_IDX + N_IDX_HEADS
    assert o_ki % LANES == 0

    n_pool = cache_kidx.shape[1]
    kidx_t = jnp.swapaxes(cache_kidx, 2, 3)
    ck_rows = cache_k.reshape(depth, n_pool * PAGE_SIZE, N_HEADS, HEAD_DIM)
    cv_rows = cache_v.reshape(depth, n_pool * PAGE_SIZE, N_HEADS, HEAD_DIM)

    w_in16 = w_in.astype(BF16)
    w_gates = w_in[:, :, o_ga:].astype(BF16)
    w_pa16, w_pb16, w_o16 = w_pa.astype(BF16), w_pb.astype(BF16), w_o.astype(BF16)
    w_gate16, w_up16, w_down16 = w_gate.astype(BF16), w_up.astype(BF16), w_down.astype(BF16)

    c_all = jnp.zeros((rs, d), F32).at[:bp].set(c_prompt).at[bp:bp + db].set(c_sample)
    mod = _mod_call(c_all, w_mod, b_mod)
    mod_p = mod[:, :bp].reshape(depth, bp, 1, 6, d)
    mod_s = jnp.zeros((depth, 1, rs, 6, d), F32).at[:, 0, :db].set(mod[:, bp:bp + db].reshape(depth, db, 6, d))

    cos_p, sin_p = _rope_tables(np.arange(tp), HEAD_DIM)
    cosi_p, sini_p = _rope_tables(np.arange(tp), D_IDX)
    cos_s, sin_s = _rope_tables(np.full((rs,), past), HEAD_DIM)
    cosi_s, sini_s = _rope_tables(np.full((rs,), past), D_IDX)

    xp = x_prompt
    xs = jnp.zeros((1, rs, d), F32).at[0, :db].set(x_sample[:, 0])
    tm_p = min(tp, 512)
    tk_ffn = w_down.shape[1] // 4

    hp = _norm_mod_call(xp, g_mix[0], mod_p[0, :, :, 1], mod_p[0, :, :, 0], tm_p)
    hs = _norm_mod_call(xs, g_mix[0], mod_s[0, :, :, 1], mod_s[0, :, :, 0], rs)

    outs = {n: [] for n in ("kip", "cp", "ks", "vs", "kis", "cs")}
    kp_all = jnp.zeros((depth, bp, tp, N_HEADS, HEAD_DIM), F32)
    vp_all = jnp.zeros((depth, bp, tp, N_HEADS, HEAD_DIM), F32)
    yp = ys = None
    for l in range(depth):
        last = l == depth - 1
        g_next = g_final if last else g_mix[l + 1]
        ln = l if last else l + 1

        sh1, sc1, gt1, sh2, sc2, gt2 = (mod_p[l, :, :, j] for j in range(6))
        a_in, conv_st = _conv_seq_call(hp, w_in16, l, conv_w[l], jnp.zeros((bp, CONV_W - 1, d_conv), F32), tm_p, 512)
        q = _proj_call(hp, w_in16, l, cb_q, att_dim, cos_p, sin_p, tm_p, half=HEAD_DIM // 2, outs=("bf16",))[0]
        kp_all, k16 = _proj_call(hp, w_in16, l, cb_k, att_dim, cos_p, sin_p, tm_p, half=HEAD_DIM // 2,
                                 outs=("heads", "bf16"), cache=kp_all)
        vp_all, v_t = _proj_call(hp, w_in16, l, cb_v, att_dim, cos_p, sin_p, tm_p, half=0,
                                 outs=("heads", "chunk_t"), tck=blk, cache=vp_all)
        qi_hm = _proj_call(hp, w_in16, l, cb_qi, idx_q, cosi_p, sini_p, tm_p, half=D_IDX // 2, outs=("idx_heads",))[0]
        kw, ki16, wi_t = _kidx_call(hp, w_in16, l, o_ki // LANES, cosi_p, sini_p, tm_p)
        att = _dsa_call(q, qi_hm, wi_t, k16, v_t, ki16, topk_p, blk)
        merged = _merge_call(hp, a_in, att, w_gates, w_pa16, w_pb16, l, tm_p, 512)
        xp, h2 = _resid_norm_call(xp, merged, w_o16, l, gt1, g_ffn[l], sc2, sh2, min(tm_p, 256))
        act = _ffn_up_call(h2, w_gate16, w_up16, l, tm_p, 512)
        xp, nxt = _ffn_down_call(xp, act, w_down16, l, gt2, g_next, mod_p[ln, :, :, 1], mod_p[ln, :, :, 0],
                                 tm_p, tk_ffn, last)
        hp, yp = (None, nxt) if last else (nxt, None)
        outs["kip"].append(kw[:, :, :D_IDX].reshape(bp, tp // PAGE_SIZE, PAGE_SIZE, D_IDX))
        outs["cp"].append(conv_st)

        sh1, sc1, gt1, sh2, sc2, gt2 = (mod_s[l, :, :, j] for j in range(6))
        prev = jnp.zeros((rs, CONV_W - 1, d_conv), F32).at[:db].set(state_conv[l])
        a_in, u = _conv_step_call(hs[0], w_in16, l, conv_w[l], prev[:, 0], prev[:, 1], 512)
        q32 = _proj_call(hs, w_in16, l, cb_q, att_dim, cos_s, sin_s, rs, half=HEAD_DIM // 2, outs=("f32",))[0]
        k32 = _proj_call(hs, w_in16, l, cb_k, att_dim, cos_s, sin_s, rs, half=HEAD_DIM // 2, outs=("f32",))[0]
        v32 = _proj_call(hs, w_in16, l, cb_v, att_dim, cos_s, sin_s, rs, half=0, outs=("f32",))[0]
        qi_hm = _proj_call(hs, w_in16, l, cb_qi, idx_q, cosi_s, sini_s, rs, half=D_IDX // 2, outs=("idx_heads",))[0]
        kw, ki16, _ = _kidx_call(hs, w_in16, l, o_ki // LANES, cosi_s, sini_s, rs)
        qi_hb = qi_hm[0, :, :db].reshape(N_IDX_HEADS * db, D_IDX)
        w_hb = kw[0, :db, D_IDX:D_IDX + N_IDX_HEADS].T.reshape(N_IDX_HEADS * db, 1)
        scores = _pool_scores_call(qi_hb, w_hb, kidx_t, l, db, _pages_per_step(n_pool))
        rows = _select_call(page_table, scores.reshape(db, n_pool, PAGE_SIZE), qi_hb, w_hb, ki16[0, :db], topk_s)
        att = _gather_attend_call(rows.reshape(db, topk_s),
                                  q32[0, :db].reshape(db, N_HEADS, HEAD_DIM),
                                  k32[0, :db].reshape(db, N_HEADS, HEAD_DIM),
                                  v32[0, :db].reshape(db, N_HEADS, HEAD_DIM),
                                  ck_rows, cv_rows, l, topk_s)
        att = jnp.zeros((1, rs, att_dim), BF16).at[0, :db].set(att.reshape(db, att_dim).astype(BF16))
        merged = _merge_call(hs, a_in[None], att, w_gates, w_pa16, w_pb16, l, rs, 512)
        xs, h2 = _resid_norm_call(xs, merged, w_o16, l, gt1, g_ffn[l], sc2, sh2, rs)
        act = _ffn_up_call(h2, w_gate16, w_up16, l, rs, 512)
        xs, nxt = _ffn_down_call(xs, act, w_down16, l, gt2, g_next, mod_s[ln, :, :, 1], mod_s[ln, :, :, 0],
                                 rs, tk_ffn, last)
        hs, ys = (None, nxt) if last else (nxt, None)
        outs["ks"].append(k32[0, :db].reshape(db, ts, N_HEADS, HEAD_DIM))
        outs["vs"].append(v32[0, :db].reshape(db, ts, N_HEADS, HEAD_DIM))
        outs["kis"].append(kw[0, :db, :D_IDX].reshape(db, ts, D_IDX))
        outs["cs"].append(jnp.stack([prev[:db, 1], u[:db]], axis=1))

    page_shape = (depth, bp, tp // PAGE_SIZE, PAGE_SIZE, N_HEADS, HEAD_DIM)
    return (yp, ys[0, :db].reshape(db, ts, d),
            kp_all.reshape(page_shape), vp_all.reshape(page_shape), jnp.stack(outs["kip"]), jnp.stack(outs["cp"]),
            jnp.stack(outs["ks"]), jnp.stack(outs["vs"]), jnp.stack(outs["kis"]), jnp.stack(outs["cs"]))
```

```python
import functools

import jax
import jax.numpy as jnp
import numpy as np
from jax import lax
from jax.experimental import pallas as pl
from jax.experimental.pallas import tpu as pltpu

F32 = jnp.float32
BF16 = jnp.bfloat16
I32 = jnp.int32
I16 = jnp.int16

N_HEADS = 8
HEAD_DIM = 128
N_IDX_HEADS = 16
D_IDX = 64
CONV_W = 3
TOPK_MAX = 256
PAGE_SIZE = 128
ROPE_THETA = 10000.0
EPS = 1e-6
INDEX_SCALE = (D_IDX ** -0.5) * (N_IDX_HEADS ** -0.5)
ATT_SCALE = HEAD_DIM ** -0.5
EXP2_SCALE = ATT_SCALE * 1.4426950408889634

LANES = 128
SUBLANES = 8
VMEM_LIMIT = 56 * 1024 * 1024
INT_MIN = -2 ** 31
HALF16 = 2 ** 15
M_FLOOR = -1e30
ATT_BLOCK = 256
SAMPLE_ROWS = 16
ROW_TILE = 1024
ROW_TILE_FULL = 512
COL_TILE = 512


def _params(sem):
    return pltpu.CompilerParams(dimension_semantics=sem, vmem_limit_bytes=VMEM_LIMIT)


def _dot(a, b):
    return jnp.dot(a, b, preferred_element_type=F32)


def _dot_nt(a, b):
    return lax.dot_general(a, b, (((1,), (1,)), ((), ())), preferred_element_type=F32)


def _rmsnorm(x):
    return x * lax.rsqrt(jnp.mean(x * x, axis=-1, keepdims=True) + EPS)


def _sort_key(s):
    s = jnp.where(s == 0.0, 0.0, s)
    b = pltpu.bitcast(s, I32)
    return jnp.where(b < 0, b ^ 0x7FFFFFFF, b)


def _w_spec(k, tn, layer, col0=0, col_axis=None, row_axis=None):
    def index(*ids):
        return (layer, 0 if row_axis is None else ids[row_axis], col0 + (0 if col_axis is None else ids[col_axis]))
    return pl.BlockSpec((None, k, tn), index)


def _whole(a):
    return pl.BlockSpec(a.shape, lambda *ids: (0,) * a.ndim)


def _batch_vec(d, batch_axis):
    return pl.BlockSpec((None, 1, d), lambda *ids: (ids[batch_axis], 0, 0))


def _mod_kernel(c_ref, w_ref, b_ref, o_ref):
    c = c_ref[...]
    a = (c * jax.nn.sigmoid(c)).astype(BF16)
    o_ref[...] = _dot(a, w_ref[...].astype(BF16)) + b_ref[...]


def _mod_call(c_all, w_mod, b_mod):
    depth, d, n = w_mod.shape
    rows = c_all.shape[0]
    tn = 1024
    return pl.pallas_call(
        _mod_kernel,
        grid=(depth, n // tn),
        in_specs=[pl.BlockSpec((rows, d), lambda l, j: (0, 0)),
                  pl.BlockSpec((None, d, tn), lambda l, j: (l, 0, j)),
                  pl.BlockSpec((None, 1, tn), lambda l, j: (l, 0, j))],
        out_specs=pl.BlockSpec((None, rows, tn), lambda l, j: (l, 0, j)),
        out_shape=jax.ShapeDtypeStruct((depth, rows, n), F32),
        compiler_params=_params(("arbitrary", "arbitrary")),
        name="mod",
    )(c_all, w_mod, b_mod.reshape(depth, 1, n))


def _norm_mod(x, g, sc, sh):
    return (_rmsnorm(x) * g * (1.0 + sc) + sh).astype(BF16)


def _norm_mod_kernel(x_ref, xs_ref, g_ref, sc_ref, sh_ref, scs_ref, shs_ref, o_ref, os_ref):
    @pl.when((pl.program_id(0) == 0) & (pl.program_id(1) == 0))
    def _():
        os_ref[...] = _norm_mod(xs_ref[...], g_ref[...], scs_ref[...], shs_ref[...])

    o_ref[...] = _norm_mod(x_ref[...], g_ref[...], sc_ref[...], sh_ref[...])


def _norm_mod_call(x, xs, g, sc, sh, scs, shs, tm):
    gn, r, d = x.shape
    row = pl.BlockSpec((None, tm, d), lambda b, i: (b, i, 0))
    g = g.reshape(1, d)
    return pl.pallas_call(
        _norm_mod_kernel,
        grid=(gn, r // tm),
        in_specs=[row, _whole(xs), _whole(g), _batch_vec(d, 0), _batch_vec(d, 0), _whole(scs), _whole(shs)],
        out_specs=[row, _whole(xs)],
        out_shape=[jax.ShapeDtypeStruct((gn, r, d), BF16), jax.ShapeDtypeStruct(xs.shape, BF16)],
        compiler_params=_params(("arbitrary", "arbitrary")),
        name="norm_mod",
    )(x, xs, g, sc, sh, scs, shs)


def _conv_kernel(h_ref, hs_ref, wb_ref, wc_ref, wv_ref, cw_ref, prev_ref, p0_ref, p1_ref,
                 a_ref, st_ref, as_ref, us_ref, buf, w16, *, tm):
    b, i = pl.program_id(1), pl.program_id(2)
    cw = cw_ref[...]

    @pl.when((b == 0) & (i == 0))
    def _():
        for k, w_ref in enumerate((wb_ref, wc_ref, wv_ref)):
            w16[k] = w_ref[...].astype(BF16)
        hs = hs_ref[...]
        us = _dot(hs, w16[1]) * _dot(hs, w16[2])
        ys = cw[0:1] * p0_ref[...] + cw[1:2] * p1_ref[...] + cw[2:3] * us
        as_ref[...] = (_dot(hs, w16[0]) * ys).astype(BF16)
        us_ref[...] = us

    h = h_ref[...]
    bg = _dot(h, w16[0])
    u = _dot(h, w16[1]) * _dot(h, w16[2])

    @pl.when(i == 0)
    def _():
        buf[6:8, :] = prev_ref[...]

    buf[8:8 + tm, :] = u
    y = cw[0:1] * buf[6:6 + tm, :] + cw[1:2] * buf[7:7 + tm, :] + cw[2:3] * u
    a_ref[...] = (bg * y).astype(BF16)
    last2 = buf[6 + tm:8 + tm, :]
    buf[6:8, :] = last2
    st_ref[...] = last2


def _conv_call(h, hs, w_in, layer, cw, prev, p0, p1, tm, tn):
    gn, r, d = h.shape
    rs = hs.shape[0]
    n = cw.shape[1]
    nj = n // tn
    s_tile = pl.BlockSpec((rs, tn), lambda j, b, i: (0, j))
    return pl.pallas_call(
        functools.partial(_conv_kernel, tm=tm),
        grid=(nj, gn, r // tm),
        in_specs=[pl.BlockSpec((None, tm, d), lambda j, b, i: (b, i, 0)), _whole(hs),
                  _w_spec(d, tn, layer, 0, 0), _w_spec(d, tn, layer, nj, 0), _w_spec(d, tn, layer, 2 * nj, 0),
                  pl.BlockSpec((CONV_W, tn), lambda j, b, i: (0, j)),
                  pl.BlockSpec((None, CONV_W - 1, tn), lambda j, b, i: (b, 0, j)),
                  s_tile, s_tile],
        out_specs=[pl.BlockSpec((None, tm, tn), lambda j, b, i: (b, i, j)),
                   pl.BlockSpec((None, CONV_W - 1, tn), lambda j, b, i: (b, 0, j)),
                   s_tile, s_tile],
        out_shape=[jax.ShapeDtypeStruct((gn, r, n), BF16), jax.ShapeDtypeStruct((gn, CONV_W - 1, n), F32),
                   jax.ShapeDtypeStruct((rs, n), BF16), jax.ShapeDtypeStruct((rs, n), F32)],
        scratch_shapes=[pltpu.VMEM((8 + tm, tn), F32), pltpu.VMEM((3, d, tn), BF16)],
        compiler_params=_params(("arbitrary", "arbitrary", "arbitrary")),
        name="conv",
    )(h, hs, w_in, w_in, w_in, cw, prev, p0, p1)


def _rope_group(x, cos, sin, half):
    if 2 * half == LANES:
        rot = pltpu.roll(x, half, 1)
    else:
        lane = lax.broadcasted_iota(I32, x.shape, 1)
        rot = jnp.where(lane % (2 * half) < half, pltpu.roll(x, LANES - half, 1), pltpu.roll(x, half, 1))
    return x * cos + rot * sin


def _proj_emit(acc, cos, sin, half, kinds, refs, tck):
    groups = []
    for c in range(acc.shape[1] // LANES):
        x = acc[:, c * LANES:(c + 1) * LANES]
        groups.append(_rope_group(x, cos, sin, half) if half else x)
    for kind, o in zip(kinds, refs):
        for c, x in enumerate(groups):
            if kind == "f32":
                o[:, c * LANES:(c + 1) * LANES] = x
            elif kind == "bf16":
                o[:, c * LANES:(c + 1) * LANES] = x.astype(BF16)
            elif kind == "heads":
                o[:, c, :] = x
            elif kind == "idx_heads":
                xb = x.astype(BF16)
                o[2 * c] = xb[:, :D_IDX]
                o[2 * c + 1] = xb[:, D_IDX:]
            elif kind == "chunk_t":
                for k in range(x.shape[0] // tck):
                    o[k, c * LANES:(c + 1) * LANES, :] = x[k * tck:(k + 1) * tck, :].T.astype(BF16)


def _proj_kernel(*refs, half, outs, outs_s, tck, aliased):
    h_ref, hs_ref, w_ref, cos_ref, sin_ref, cos_s_ref, sin_s_ref = refs[:7]
    rest = refs[7 + aliased:]
    out_refs, out_s_refs, w16 = rest[:len(outs)], rest[len(outs):len(outs) + len(outs_s)], rest[-1]

    @pl.when((pl.program_id(0) == 0) & (pl.program_id(1) == 0))
    def _():
        w16[...] = w_ref[...].astype(BF16)
        _proj_emit(_dot(hs_ref[...], w16[...]), cos_s_ref[...], sin_s_ref[...], half, outs_s, out_s_refs, None)

    _proj_emit(_dot(h_ref[...], w16[...]), cos_ref[...], sin_ref[...], half, outs, out_refs, tck)


def _proj_call(h, hs, w_in, layer, col0, n, rope, rope_s, tm, *, half, outs, outs_s, tck=None, cache=None):
    gn, r, d = h.shape
    rs = hs.shape[0]
    heads = n // LANES
    depth = None if cache is None else cache.shape[0]
    out_specs, out_shape = [], []
    for kind in outs:
        if kind == "bf16":
            out_specs.append(pl.BlockSpec((None, tm, n), lambda b, i: (b, i, 0)))
            out_shape.append(jax.ShapeDtypeStruct((gn, r, n), BF16))
        elif kind == "heads":
            out_specs.append(pl.BlockSpec((None, None, tm, heads, LANES), lambda b, i: (layer, b, i, 0, 0)))
            out_shape.append(jax.ShapeDtypeStruct((depth, gn, r, heads, LANES), F32))
        elif kind == "idx_heads":
            out_specs.append(pl.BlockSpec((None, n // D_IDX, tm, D_IDX), lambda b, i: (b, 0, i, 0)))
            out_shape.append(jax.ShapeDtypeStruct((gn, n // D_IDX, r, D_IDX), BF16))
        elif kind == "chunk_t":
            out_specs.append(pl.BlockSpec((None, tm // tck, n, tck), lambda b, i: (b, i, 0, 0)))
            out_shape.append(jax.ShapeDtypeStruct((gn, r // tck, n, tck), BF16))
    for kind in outs_s:
        shape, dtype = ((rs, n), F32) if kind == "f32" else ((n // D_IDX, rs, D_IDX), BF16)
        out_specs.append(pl.BlockSpec(shape, lambda b, i, nd=len(shape): (0,) * nd))
        out_shape.append(jax.ShapeDtypeStruct(shape, dtype))
    tab = pl.BlockSpec((tm, LANES), lambda b, i: (i, 0))
    in_specs = [pl.BlockSpec((None, tm, d), lambda b, i: (b, i, 0)), _whole(hs), _w_spec(d, n, layer, col0),
                tab, tab, _whole(rope_s[0]), _whole(rope_s[1])]
    args = [h, hs, w_in, rope[0], rope[1], rope_s[0], rope_s[1]]
    aliases = {}
    if cache is not None:
        in_specs.append(pl.BlockSpec(memory_space=pl.ANY))
        args.append(cache)
        aliases = {7: outs.index("heads")}
    return pl.pallas_call(
        functools.partial(_proj_kernel, half=half, outs=outs, outs_s=outs_s, tck=tck, aliased=cache is not None),
        grid=(gn, r // tm),
        in_specs=in_specs, out_specs=out_specs, out_shape=out_shape,
        scratch_shapes=[pltpu.VMEM((d, n), BF16)],
        input_output_aliases=aliases,
        compiler_params=_params(("arbitrary", "arbitrary")),
        name="proj",
    )(*args)


def _kidx_rows(acc, cos, sin):
    roped = _rope_group(acc, cos, sin, D_IDX // 2)
    lane = lax.broadcasted_iota(I32, acc.shape, 1)
    return jnp.where(lane < D_IDX, roped, jnp.where(lane < D_IDX + N_IDX_HEADS, acc, 0.0))


def _kidx_kernel(h_ref, hs_ref, w_ref, cos_ref, sin_ref, cos_s_ref, sin_s_ref,
                 kw_ref, kb_ref, wt_ref, kws_ref, kbs_ref, w16):
    @pl.when((pl.program_id(0) == 0) & (pl.program_id(1) == 0))
    def _():
        w16[...] = w_ref[...].astype(BF16)
        kws = _kidx_rows(_dot(hs_ref[...], w16[...]), cos_s_ref[...], sin_s_ref[...])
        kws_ref[...] = kws
        kbs_ref[...] = kws[:, :D_IDX].astype(BF16)

    kw = _kidx_rows(_dot(h_ref[...], w16[...]), cos_ref[...], sin_ref[...])
    kw_ref[...] = kw
    kb_ref[...] = kw[:, :D_IDX].astype(BF16)
    wt_ref[...] = kw.T[D_IDX:D_IDX + N_IDX_HEADS, :]


def _kidx_call(h, hs, w_in, layer, col0, rope, rope_s, tm):
    gn, r, d = h.shape
    rs = hs.shape[0]
    tab = pl.BlockSpec((tm, LANES), lambda b, i: (i, 0))
    s128 = pl.BlockSpec((rs, LANES), lambda b, i: (0, 0))
    s64 = pl.BlockSpec((rs, D_IDX), lambda b, i: (0, 0))
    return pl.pallas_call(
        _kidx_kernel,
        grid=(gn, r // tm),
        in_specs=[pl.BlockSpec((None, tm, d), lambda b, i: (b, i, 0)), _whole(hs), _w_spec(d, LANES, layer, col0),
                  tab, tab, s128, s128],
        out_specs=[pl.BlockSpec((None, tm, LANES), lambda b, i: (b, i, 0)),
                   pl.BlockSpec((None, tm, D_IDX), lambda b, i: (b, i, 0)),
                   pl.BlockSpec((None, N_IDX_HEADS, tm), lambda b, i: (b, 0, i)),
                   s128, s64],
        out_shape=[jax.ShapeDtypeStruct((gn, r, LANES), F32), jax.ShapeDtypeStruct((gn, r, D_IDX), BF16),
                   jax.ShapeDtypeStruct((gn, N_IDX_HEADS, r), F32),
                   jax.ShapeDtypeStruct((rs, LANES), F32), jax.ShapeDtypeStruct((rs, D_IDX), BF16)],
        scratch_shapes=[pltpu.VMEM((d, LANES), BF16)],
        compiler_params=_params(("arbitrary", "arbitrary")),
        name="kidx",
    )(h, hs, w_in, rope[0], rope[1], rope_s[0], rope_s[1])


def _strict_tri(n, dtype, lower):
    row = lax.broadcasted_iota(I32, (n, n), 0)
    col = lax.broadcasted_iota(I32, (n, n), 1)
    return jnp.where(col < row if lower else row < col, 1.0, 0.0).astype(dtype)


def _dsa_kernel(q_ref, qi_ref, wi_ref, k_ref, vt_ref, ki_ref, o_ref,
                key_scr, hi_scr, lo_scr, m_scr, l_scr, acc_scr, *, topk, ck, qb):
    i = pl.program_id(1)
    nch = ((i + 1) * qb + ck - 1) // ck
    qpos = i * qb + lax.broadcasted_iota(I32, (LANES, qb), 1)
    krow = lax.broadcasted_iota(I32, (LANES, qb), 0)
    w = wi_ref[...]

    def index_chunk(c, _):
        base = pl.multiple_of(c * ck, ck)
        for r in range(ck // LANES):
            rows = slice(r * LANES, (r + 1) * LANES)
            kc = ki_ref[pl.ds(base + r * LANES, LANES), :]
            acc = jnp.zeros((LANES, qb), F32)
            for h in range(N_IDX_HEADS):
                acc = acc + jnp.maximum(_dot_nt(kc, qi_ref[h]), 0.0) * w[h:h + 1, :]
            kpos = base + r * LANES + krow
            key = jnp.where(kpos <= qpos, _sort_key(acc * INDEX_SCALE), INT_MIN)
            key_scr[c, rows, :] = key
            hi_scr[c, rows, :] = lax.shift_right_arithmetic(key, 16).astype(I16)
            lo_scr[c, rows, :] = ((key & 0xFFFF) - HALF16).astype(I16)
        return 0

    lax.fori_loop(0, nch, index_chunk, 0)

    ge = lambda a, b: a >= b
    gt = lambda a, b: a > b
    pack = 2 * SUBLANES

    def count16(scr, pred, thr):
        thr = thr.astype(I16)

        def body(c, part):
            hit = jnp.where(pred(scr[c], thr), jnp.ones((), BF16), jnp.zeros((), BF16))
            terms = [hit[k * pack:(k + 1) * pack] for k in range(ck // pack)]
            while len(terms) > 1:
                terms = [a + b for a, b in zip(terms[0::2], terms[1::2])]
            return part + terms[0]

        part = lax.fori_loop(0, nch, body, jnp.zeros((pack, qb), BF16))
        return jnp.sum(part.astype(F32), axis=0, keepdims=True).astype(I32)

    def search16(scr, need):
        def step(it, t):
            cand = t + jnp.left_shift(jnp.int32(1), 15 - it)
            return jnp.where(count16(scr, ge, cand) >= need, cand, t)
        return lax.fori_loop(0, 16, step, jnp.full((1, qb), -HALF16, I32))

    t_hi = search16(hi_scr, topk)
    need_lo = topk - count16(hi_scr, gt, t_hi)
    t_hi16 = t_hi.astype(I16)

    def keep_band(c, _):
        lo_scr[c] = jnp.where(hi_scr[c] == t_hi16, lo_scr[c], jnp.full((), -HALF16, I16))
        return 0

    lax.fori_loop(0, nch, keep_band, 0)
    t_lo = search16(lo_scr, need_lo)
    t = t_hi * (2 * HALF16) + (t_lo + HALF16)
    t = jnp.maximum(t, INT_MIN + 1)

    def count(pred, thr):
        def body(c, cnt):
            hit = jnp.where(pred(key_scr[c], thr), 1, 0)
            return cnt + jnp.sum(hit.reshape(ck // SUBLANES, SUBLANES, qb), axis=0)
        cnt = lax.fori_loop(0, nch, body, jnp.zeros((SUBLANES, qb), I32))
        return jnp.sum(cnt, axis=0, keepdims=True)

    c_ge = count(ge, t)

    @pl.when(jnp.max(c_ge) > topk)
    def _():
        need = (topk - count(gt, t)).astype(F32)
        lower = _strict_tri(ck, BF16, lower=True)

        def body(c, carry):
            kc = key_scr[c]
            eq = kc == t
            eqf = jnp.where(eq, 1.0, 0.0)
            before = _dot(lower, eqf.astype(BF16)) + carry
            key_scr[c] = jnp.where(eq & (before >= need), INT_MIN, kc)
            return carry + jnp.sum(eqf, axis=0, keepdims=True)

        lax.fori_loop(0, nch, body, jnp.zeros((1, qb), F32))

    m_scr[...] = jnp.full(m_scr.shape, M_FLOOR, F32)
    l_scr[...] = jnp.zeros(l_scr.shape, F32)
    acc_scr[...] = jnp.zeros(acc_scr.shape, F32)

    def attend_chunk(c, _):
        base = pl.multiple_of(c * ck, ck)
        sel = key_scr[c] >= t
        heads = [slice(h * HEAD_DIM, (h + 1) * HEAD_DIM) for h in range(N_HEADS)]
        raw = [_dot_nt(k_ref[pl.ds(base, ck), cols], q_ref[:, cols]) for cols in heads]
        probs = []
        for h in range(N_HEADS):
            s = jnp.where(sel, raw[h], -jnp.inf)
            m_old = m_scr[h]
            m_new = jnp.maximum(m_old, jnp.max(s, axis=0, keepdims=True))
            alpha = jnp.exp2((m_old - m_new) * EXP2_SCALE)
            p = jnp.exp2((s - m_new) * EXP2_SCALE)
            l_scr[h] = alpha * l_scr[h] + jnp.sum(p, axis=0, keepdims=True)
            m_scr[h] = m_new
            probs.append((alpha, p.astype(BF16)))
        for h in range(N_HEADS):
            alpha, p = probs[h]
            acc_scr[h] = alpha * acc_scr[h] + _dot(vt_ref[c, heads[h], :], p)
        return 0

    lax.fori_loop(0, nch, attend_chunk, 0)
    for h in range(N_HEADS):
        o_ref[:, h * HEAD_DIM:(h + 1) * HEAD_DIM] = (acc_scr[h] / l_scr[h]).T.astype(o_ref.dtype)


def _dsa_call(q, qi_hm, wi_t, k, v_t, ki, topk, blk):
    bn, t, a = q.shape
    assert t <= 256 * 2 * SUBLANES, "the threshold search counts in bf16: at most 256 keys per packed-row entry"
    full = lambda shape: pl.BlockSpec((None,) + shape, lambda b, i: (b,) + (0,) * len(shape))
    return pl.pallas_call(
        functools.partial(_dsa_kernel, topk=topk, ck=blk, qb=blk),
        grid=(bn, t // blk),
        in_specs=[pl.BlockSpec((None, blk, a), lambda b, i: (b, i, 0)),
                  pl.BlockSpec((None, N_IDX_HEADS, blk, D_IDX), lambda b, i: (b, 0, i, 0)),
                  pl.BlockSpec((None, N_IDX_HEADS, blk), lambda b, i: (b, 0, i)),
                  full((t, a)), full((t // blk, a, blk)), full((t, D_IDX))],
        out_specs=pl.BlockSpec((None, blk, a), lambda b, i: (b, i, 0)),
        out_shape=jax.ShapeDtypeStruct((bn, t, a), BF16),
        scratch_shapes=[pltpu.VMEM((t // blk, blk, blk), I32),
                        pltpu.VMEM((t // blk, blk, blk), I16),
                        pltpu.VMEM((t // blk, blk, blk), I16),
                        pltpu.VMEM((N_HEADS, 1, blk), F32),
                        pltpu.VMEM((N_HEADS, 1, blk), F32),
                        pltpu.VMEM((N_HEADS, HEAD_DIM, blk), F32)],
        compiler_params=_params(("arbitrary", "arbitrary")),
        name="dsa_prompt",
    )(q, qi_hm, wi_t, k, v_t, ki)


def _pool_scores_kernel(qi_ref, w_ref, ck_ref, o_ref, *, nb, group):
    pages, _, page = ck_ref.shape
    qi, w = qi_ref[...], w_ref[...]
    for p in range(0, pages, group):
        kc = jnp.concatenate([ck_ref[p + g] for g in range(group)], axis=1).astype(BF16)
        d = jnp.maximum(_dot(qi, kc), 0.0) * w
        s = d[0:nb]
        for h in range(1, N_IDX_HEADS):
            s = s + d[h * nb:(h + 1) * nb]
        o_ref[:, p * page:(p + group) * page] = s * INDEX_SCALE


def _pool_scores_call(qi_hb, w_hb, kidx_t, layer, nb, pages_per_step):
    depth, n_pool, di, page = kidx_t.shape
    group = 2 if pages_per_step % 2 == 0 else 1
    return pl.pallas_call(
        functools.partial(_pool_scores_kernel, nb=nb, group=group),
        grid=(n_pool // pages_per_step,),
        in_specs=[pl.BlockSpec(qi_hb.shape, lambda j: (0, 0)),
                  pl.BlockSpec(w_hb.shape, lambda j: (0, 0)),
                  pl.BlockSpec((None, pages_per_step, di, page), lambda j: (layer, j, 0, 0))],
        out_specs=pl.BlockSpec((nb, pages_per_step * page), lambda j: (0, j)),
        out_shape=jax.ShapeDtypeStruct((nb, n_pool * page), F32),
        compiler_params=_params(("arbitrary",)),
        name="pool_scores",
    )(qi_hb, w_hb, kidx_t)


def _select_kernel(pt_ref, sc_ref, ptv_ref, qi_ref, w_ref, kn_ref, row_ref, s_scr, *, topk, n_pages, nb):
    def gather_page(p, _):
        for b in range(nb):
            s_scr[b, pl.ds(p, 1), :] = sc_ref[b, pl.ds(pt_ref[b, p], 1), :]
        return 0

    lax.fori_loop(0, n_pages, gather_page, 0)
    key = _sort_key(s_scr[...])

    def total(x):
        return jnp.sum(jnp.sum(x, axis=1, keepdims=True), axis=2, keepdims=True)

    d = jnp.maximum(_dot_nt(qi_ref[...], kn_ref[...]), 0.0) * w_ref[...]
    row = lax.broadcasted_iota(I32, d.shape, 0)
    col = lax.broadcasted_iota(I32, d.shape, 1)
    s_new = jnp.stack([jnp.sum(jnp.where((row % nb == b) & (col == b), d, 0.0), keepdims=True) for b in range(nb)])
    key_new = _sort_key(s_new * INDEX_SCALE)

    def search(it, t):
        cand = t + jnp.left_shift(jnp.int32(1), 31 - it)
        n_ge = total(jnp.where(key >= cand, 1, 0)) + jnp.where(key_new >= cand, 1, 0)
        return jnp.where(n_ge >= topk, cand, t)

    t = lax.fori_loop(0, 32, search, jnp.full((nb, 1, 1), INT_MIN, I32))
    t = jnp.maximum(t, INT_MIN + 1)

    upper = _strict_tri(PAGE_SIZE, BF16, lower=False)
    lower = _strict_tri(n_pages, BF16, lower=True)
    ones = jnp.ones((SUBLANES, PAGE_SIZE), BF16)
    incl = jnp.where(lax.broadcasted_iota(I32, (n_pages, n_pages), 0) <= lax.broadcasted_iota(I32, (n_pages, n_pages), 1),
                     1.0, 0.0).astype(BF16)
    slot = lax.broadcasted_iota(I32, (topk, PAGE_SIZE), 0).astype(F32)
    lane = lax.broadcasted_iota(I32, (topk, PAGE_SIZE), 1)
    page_lane = lax.broadcasted_iota(I32, (topk, n_pages), 1)

    def ranks(f16):
        per_page = jnp.broadcast_to(jnp.sum(f16.astype(F32), axis=1, keepdims=True), f16.shape)
        return _dot(f16, upper), _dot(lower, per_page.astype(BF16))

    for b in range(nb):
        kb, tb, knb = key[b], t[b], key_new[b]
        n_gt = jnp.sum(jnp.where(kb > tb, 1.0, 0.0), keepdims=True) + jnp.where(knb > tb, 1.0, 0.0)
        need = topk - n_gt
        eq = kb == tb
        eq16 = jnp.where(eq, 1.0, 0.0).astype(BF16)
        eq_in, eq_before = ranks(eq16)
        sel = (kb > tb) | (eq & (eq_in + eq_before < need))
        sel_new = (knb > tb) | ((knb == tb) & (jnp.sum(eq16.astype(F32), keepdims=True) < need))

        sel16 = jnp.where(sel, 1.0, 0.0).astype(BF16)
        in_page, before_page = ranks(sel16)
        n_sel = jnp.sum(sel16.astype(F32), keepdims=True)
        cum = _dot(_dot_nt(ones, sel16).astype(BF16), incl)[0:1, :]
        page_r = jnp.sum(jnp.where(cum <= slot[:, 0:1], 1.0, 0.0), axis=1, keepdims=True).astype(I32)
        hit = page_lane == page_r
        onehot = jnp.where(hit, 1.0, 0.0).astype(BF16)
        got = _dot(onehot, jnp.where(sel, in_page, -1.0).astype(BF16))
        within = slot - _dot(onehot, before_page.astype(BF16))
        off = jnp.sum(jnp.where(got == within, lane, 0), axis=1, keepdims=True)
        phys = jnp.sum(jnp.where(hit, ptv_ref[b:b + 1, :], 0), axis=1, keepdims=True)
        rows = jnp.where(slot[:, 0:1] < n_sel, phys * PAGE_SIZE + off, 0)
        row_ref[b] = rows - jnp.where(sel_new & (slot[:, 0:1] == n_sel), 1, 0)


def _select_call(page_table, scores, qi_hb, w_hb, k_new, topk):
    nb, n_pages = page_table.shape
    whole = lambda a: pl.BlockSpec(a.shape, lambda i, pt: (0,) * a.ndim)
    return pl.pallas_call(
        functools.partial(_select_kernel, topk=topk, n_pages=n_pages, nb=nb),
        grid_spec=pltpu.PrefetchScalarGridSpec(
            num_scalar_prefetch=1, grid=(1,),
            in_specs=[whole(scores), whole(page_table), whole(qi_hb), whole(w_hb), whole(k_new)],
            out_specs=pl.BlockSpec((nb, topk, 1), lambda i, pt: (0, 0, 0)),
            scratch_shapes=[pltpu.VMEM((nb, n_pages, PAGE_SIZE), F32)]),
        out_shape=jax.ShapeDtypeStruct((nb, topk, 1), I32),
        compiler_params=_params(("arbitrary",)),
        name="select",
    )(page_table, scores, page_table, qi_hb, w_hb, k_new)


def _gather_attend_kernel(row_ref, q_ref, kn_ref, vn_ref, ck_hbm, cv_hbm, o_ref, kbuf, vbuf, sem, *, layer, topk):
    b = pl.program_id(0)

    def copies(r):
        src = jnp.maximum(row_ref[b, r], 0)
        return (pltpu.make_async_copy(ck_hbm.at[layer, src], kbuf.at[r], sem.at[0]),
                pltpu.make_async_copy(cv_hbm.at[layer, src], vbuf.at[r], sem.at[1]))

    def start(r, _):
        for cp in copies(r):
            cp.start()
        return 0

    def wait(r, _):
        for cp in copies(r):
            cp.wait()
        return 0

    lax.fori_loop(0, topk, start, 0)
    lax.fori_loop(0, topk, wait, 0)

    @pl.when(row_ref[b, topk - 1] < 0)
    def _():
        kbuf[topk - 1] = kn_ref[...]
        vbuf[topk - 1] = vn_ref[...]

    q = q_ref[...]
    s = jnp.sum(kbuf[...] * q[None], axis=-1, keepdims=True) * ATT_SCALE
    p = jnp.exp(s - jnp.max(s, axis=0, keepdims=True))
    p = p / jnp.sum(p, axis=0, keepdims=True)
    o_ref[...] = jnp.sum(p * vbuf[...], axis=0)


def _gather_attend_call(rows, q, k_new, v_new, cache_k, cache_v, layer, topk):
    nb = rows.shape[0]
    any_spec = pl.BlockSpec(memory_space=pl.ANY)
    per_sample = pl.BlockSpec((None, N_HEADS, HEAD_DIM), lambda b, rows: (b, 0, 0))
    return pl.pallas_call(
        functools.partial(_gather_attend_kernel, layer=layer, topk=topk),
        grid_spec=pltpu.PrefetchScalarGridSpec(
            num_scalar_prefetch=1, grid=(nb,),
            in_specs=[per_sample, per_sample, per_sample, any_spec, any_spec],
            out_specs=per_sample,
            scratch_shapes=[pltpu.VMEM((topk, N_HEADS, HEAD_DIM), F32),
                            pltpu.VMEM((topk, N_HEADS, HEAD_DIM), F32),
                            pltpu.SemaphoreType.DMA((2,))]),
        out_shape=jax.ShapeDtypeStruct((nb, N_HEADS, HEAD_DIM), F32),
        compiler_params=_params(("arbitrary",)),
        name="gather_attend",
    )(rows, q, k_new, v_new, cache_k, cache_v)


def _merge_kernel(h_ref, a_ref, t_ref, hs_ref, as_ref, ts_ref, wga_ref, wgb_ref, wpa_ref, wpb_ref,
                  o_ref, os_ref, wpa16, wpb16):
    def mix(h, a, t):
        ya = _dot(a, wpa16[...])
        yb = _dot(t, wpb16[...])
        ga = jax.nn.sigmoid(_dot(h, wga_ref[...]))
        gb = jax.nn.sigmoid(_dot(h, wgb_ref[...]))
        return (ga * ya + gb * yb).astype(BF16)

    @pl.when((pl.program_id(1) == 0) & (pl.program_id(2) == 0))
    def _():
        wpa16[...] = wpa_ref[...].astype(BF16)
        wpb16[...] = wpb_ref[...].astype(BF16)
        os_ref[...] = mix(hs_ref[...], as_ref[...], ts_ref[...])

    o_ref[...] = mix(h_ref[...], a_ref[...], t_ref[...])


def _merge_call(h, a_in, att, hs, as_in, att_s, w_gates, w_pa, w_pb, layer, tm, tn):
    gn, r, d = h.shape
    a = a_in.shape[2]
    rs = hs.shape[0]
    nj = d // tn
    rows = lambda w: pl.BlockSpec((None, tm, w), lambda j, b, i: (b, i, 0))
    s_out = pl.BlockSpec((rs, tn), lambda j, b, i: (0, j))
    return pl.pallas_call(
        _merge_kernel,
        grid=(nj, gn, r // tm),
        in_specs=[rows(d), rows(a), rows(a), _whole(hs), _whole(as_in), _whole(att_s),
                  _w_spec(d, tn, layer, 0, 0), _w_spec(d, tn, layer, nj, 0),
                  _w_spec(a, tn, layer, 0, 0), _w_spec(a, tn, layer, 0, 0)],
        out_specs=[pl.BlockSpec((None, tm, tn), lambda j, b, i: (b, i, j)), s_out],
        out_shape=[jax.ShapeDtypeStruct((gn, r, d), BF16), jax.ShapeDtypeStruct((rs, d), BF16)],
        scratch_shapes=[pltpu.VMEM((a, tn), BF16), pltpu.VMEM((a, tn), BF16)],
        compiler_params=_params(("arbitrary", "arbitrary", "arbitrary")),
        name="merge",
    )(h, a_in, att, hs, as_in, att_s, w_gates, w_gates, w_pa, w_pb)


def _resid_norm_kernel(x_ref, m_ref, xs_ref, ms_ref, w_ref, g_ref, gt_ref, sc_ref, sh_ref, gts_ref, scs_ref, shs_ref,
                       x_out, h_out, xs_out, hs_out):
    def step(x, m, gt, sc, sh):
        x = x + gt * _dot(m, w_ref[...])
        return x, _norm_mod(x, g_ref[...], sc, sh)

    @pl.when((pl.program_id(0) == 0) & (pl.program_id(1) == 0))
    def _():
        xs_out[...], hs_out[...] = step(xs_ref[...], ms_ref[...], gts_ref[...], scs_ref[...], shs_ref[...])

    x_out[...], h_out[...] = step(x_ref[...], m_ref[...], gt_ref[...], sc_ref[...], sh_ref[...])


def _resid_norm_call(x, m, xs, ms, w_o, layer, g, gt, sc, sh, gts, scs, shs, tm):
    gn, r, d = x.shape
    row = pl.BlockSpec((None, tm, d), lambda b, i: (b, i, 0))
    g = g.reshape(1, d)
    return pl.pallas_call(
        _resid_norm_kernel,
        grid=(gn, r // tm),
        in_specs=[row, row, _whole(xs), _whole(ms), _w_spec(d, d, layer), _whole(g),
                  _batch_vec(d, 0), _batch_vec(d, 0), _batch_vec(d, 0), _whole(gts), _whole(scs), _whole(shs)],
        out_specs=[row, row, _whole(xs), _whole(xs)],
        out_shape=[jax.ShapeDtypeStruct((gn, r, d), F32), jax.ShapeDtypeStruct((gn, r, d), BF16),
                   jax.ShapeDtypeStruct(xs.shape, F32), jax.ShapeDtypeStruct(xs.shape, BF16)],
        compiler_params=_params(("arbitrary", "arbitrary")),
        name="resid_norm",
    )(x, m, xs, ms, w_o, g, gt, sc, sh, gts, scs, shs)


def _ffn_up_kernel(h_ref, hs_ref, wg_ref, wu_ref, o_ref, os_ref, wg16, wu16):
    def act(h):
        g = _dot(h, wg16[...])
        return (g * jax.nn.sigmoid(g) * _dot(h, wu16[...])).astype(BF16)

    @pl.when((pl.program_id(1) == 0) & (pl.program_id(2) == 0))
    def _():
        wg16[...] = wg_ref[...].astype(BF16)
        wu16[...] = wu_ref[...].astype(BF16)
        os_ref[...] = act(hs_ref[...])

    o_ref[...] = act(h_ref[...])


def _ffn_up_call(h, hs, w_gate, w_up, layer, tm, tn):
    gn, r, d = h.shape
    rs = hs.shape[0]
    f = w_gate.shape[2]
    return pl.pallas_call(
        _ffn_up_kernel,
        grid=(f // tn, gn, r // tm),
        in_specs=[pl.BlockSpec((None, tm, d), lambda j, b, i: (b, i, 0)), _whole(hs),
                  _w_spec(d, tn, layer, 0, 0), _w_spec(d, tn, layer, 0, 0)],
        out_specs=[pl.BlockSpec((None, tm, tn), lambda j, b, i: (b, i, j)),
                   pl.BlockSpec((rs, tn), lambda j, b, i: (0, j))],
        out_shape=[jax.ShapeDtypeStruct((gn, r, f), BF16), jax.ShapeDtypeStruct((rs, f), BF16)],
        scratch_shapes=[pltpu.VMEM((d, tn), BF16), pltpu.VMEM((d, tn), BF16)],
        compiler_params=_params(("arbitrary", "arbitrary", "arbitrary")),
        name="ffn_up",
    )(h, hs, w_gate, w_up)


def _ffn_down_kernel(x_ref, a_ref, xs_ref, as_ref, w_ref, g_ref, gt_ref, sc_ref, sh_ref, gts_ref, scs_ref, shs_ref,
                     x_out, n_out, xs_out, ns_out, acc, acc_s, *, final):
    k = pl.program_id(2)
    last = k == pl.num_programs(2) - 1
    with_samples = (pl.program_id(0) == 0) & (pl.program_id(1) == 0)

    def finish(x, gt, sc, sh, total, x_o, n_o):
        x = x + gt * total
        x_o[...] = x
        n_o[...] = (_rmsnorm(x) * g_ref[...]).astype(n_o.dtype) if final else _norm_mod(x, g_ref[...], sc, sh)

    @pl.when(k == 0)
    def _():
        acc[...] = jnp.zeros(acc.shape, F32)

    acc[...] += _dot(a_ref[...], w_ref[...])

    @pl.when(with_samples & (k == 0))
    def _():
        acc_s[...] = jnp.zeros(acc_s.shape, F32)

    @pl.when(with_samples)
    def _():
        acc_s[...] += _dot(as_ref[...], w_ref[...])

    @pl.when(with_samples & last)
    def _():
        finish(xs_ref[...], gts_ref[...], scs_ref[...], shs_ref[...], acc_s[...], xs_out, ns_out)

    @pl.when(last)
    def _():
        finish(x_ref[...], gt_ref[...], sc_ref[...], sh_ref[...], acc[...], x_out, n_out)


def _ffn_down_call(x, act, xs, act_s, w_down, layer, g, gt, sc, sh, gts, scs, shs, tm, tk, final):
    gn, r, d = x.shape
    rs = xs.shape[0]
    f = act.shape[2]
    row = pl.BlockSpec((None, tm, d), lambda b, i, k: (b, i, 0))
    g = g.reshape(1, d)
    n_dtype = F32 if final else BF16
    return pl.pallas_call(
        functools.partial(_ffn_down_kernel, final=final),
        grid=(gn, r // tm, f // tk),
        in_specs=[row, pl.BlockSpec((None, tm, tk), lambda b, i, k: (b, i, k)),
                  _whole(xs), pl.BlockSpec((rs, tk), lambda b, i, k: (0, k)),
                  _w_spec(tk, d, layer, row_axis=2), _whole(g),
                  _batch_vec(d, 0), _batch_vec(d, 0), _batch_vec(d, 0), _whole(gts), _whole(scs), _whole(shs)],
        out_specs=[row, row, _whole(xs), _whole(xs)],
        out_shape=[jax.ShapeDtypeStruct((gn, r, d), F32), jax.ShapeDtypeStruct((gn, r, d), n_dtype),
                   jax.ShapeDtypeStruct(xs.shape, F32), jax.ShapeDtypeStruct(xs.shape, n_dtype)],
        scratch_shapes=[pltpu.VMEM((tm, d), F32), pltpu.VMEM((rs, d), F32)],
        compiler_params=_params(("arbitrary", "arbitrary", "arbitrary")),
        name="ffn_down",
    )(x, act, xs, act_s, w_down, g, gt, sc, sh, gts, scs, shs)


def _rope_tables(pos, d):
    inv_freq = ROPE_THETA ** (-np.arange(0, d, 2, dtype=np.float64) / d)
    ang = np.asarray(pos, np.float64)[:, None] * inv_freq[None, :]
    cos, sin = np.cos(ang), np.sin(ang)
    reps = LANES // d
    return (jnp.asarray(np.tile(np.concatenate([cos, cos], axis=1), (1, reps)), F32),
            jnp.asarray(np.tile(np.concatenate([-sin, sin], axis=1), (1, reps)), F32))


def _pages_per_step(n_pool):
    return max(p for p in range(1, 33) if n_pool % p == 0)


def kernel(x_prompt, x_sample, c_prompt, c_sample, cache_k, cache_v, cache_kidx, state_conv, page_table,
           w_mod, b_mod, g_mix, g_ffn, w_in, conv_w, w_pa, w_pb, w_o, w_gate, w_up, w_down, g_final):
    bp, tp, d = x_prompt.shape
    db, ts, _ = x_sample.shape
    assert ts == 1, "the sample path handles one new token per sequence"
    depth = w_mod.shape[0]
    n_pages = page_table.shape[1]
    past = n_pages * PAGE_SIZE
    att_dim = N_HEADS * HEAD_DIM
    d_conv = d // 2
    idx_q = N_IDX_HEADS * D_IDX
    topk_p = min(TOPK_MAX, tp // 4)
    topk_s = min(TOPK_MAX, (past + ts) // 4)
    rs = SAMPLE_ROWS
    blk = ATT_BLOCK if tp % ATT_BLOCK == 0 else LANES
    tm, tm_full, tn = min(tp, ROW_TILE), min(tp, ROW_TILE_FULL), COL_TILE
    assert db <= rs and tp % blk == 0 and tp % tm == 0 and d_conv == att_dim == idx_q

    cb_q, cb_k, cb_v, cb_qi = 3, 4, 5, 6
    o_ki = 7 * att_dim
    o_ga = o_ki + D_IDX + N_IDX_HEADS
    assert o_ki % LANES == 0

    n_pool = cache_kidx.shape[1]
    kidx_t = jnp.swapaxes(cache_kidx, 2, 3)
    ck_rows = cache_k.reshape(depth, n_pool * PAGE_SIZE, N_HEADS, HEAD_DIM)
    cv_rows = cache_v.reshape(depth, n_pool * PAGE_SIZE, N_HEADS, HEAD_DIM)

    w_gates = w_in[:, :, o_ga:].astype(BF16)
    w_o16, w_down16 = w_o.astype(BF16), w_down.astype(BF16)

    c_all = jnp.zeros((rs, d), F32).at[:bp].set(c_prompt).at[bp:bp + db].set(c_sample)
    mod = _mod_call(c_all, w_mod, b_mod)
    mod_p = mod[:, :bp].reshape(depth, bp, 1, 6, d)
    mod_s = jnp.zeros((depth, rs, 6, d), F32).at[:, :db].set(mod[:, bp:bp + db].reshape(depth, db, 6, d))

    rope_p, ropei_p = _rope_tables(np.arange(tp), HEAD_DIM), _rope_tables(np.arange(tp), D_IDX)
    rope_s, ropei_s = _rope_tables(np.full((rs,), past), HEAD_DIM), _rope_tables(np.full((rs,), past), D_IDX)

    xp = x_prompt
    xs = jnp.zeros((rs, d), F32).at[:db].set(x_sample[:, 0])
    tk_ffn = w_down.shape[1] // 4

    hp, hs = _norm_mod_call(xp, xs, g_mix[0], mod_p[0, :, :, 1], mod_p[0, :, :, 0], mod_s[0, :, 1], mod_s[0, :, 0], tm_full)

    outs = {n: [] for n in ("kip", "cp", "ks", "vs", "kis", "cs")}
    kp_all = jnp.zeros((depth, bp, tp, N_HEADS, HEAD_DIM), F32)
    vp_all = jnp.zeros((depth, bp, tp, N_HEADS, HEAD_DIM), F32)
    yp = ys = None
    for l in range(depth):
        last = l == depth - 1
        g_next = g_final if last else g_mix[l + 1]
        ln = l if last else l + 1
        sh1, sc1, gt1, sh2, sc2, gt2 = (mod_p[l, :, :, j] for j in range(6))
        sh1s, sc1s, gt1s, sh2s, sc2s, gt2s = (mod_s[l, :, j] for j in range(6))
        prev_s = jnp.zeros((rs, CONV_W - 1, d_conv), F32).at[:db].set(state_conv[l])

        a_in, conv_st, a_in_s, u_s = _conv_call(hp, hs, w_in, l, conv_w[l], jnp.zeros((bp, CONV_W - 1, d_conv), F32),
                                                prev_s[:, 0], prev_s[:, 1], tm, tn)
        q, q_s = _proj_call(hp, hs, w_in, l, cb_q, att_dim, rope_p, rope_s, tm, half=HEAD_DIM // 2,
                            outs=("bf16",), outs_s=("f32",))
        kp_all, k16, k_s = _proj_call(hp, hs, w_in, l, cb_k, att_dim, rope_p, rope_s, tm, half=HEAD_DIM // 2,
                                      outs=("heads", "bf16"), outs_s=("f32",), cache=kp_all)
        vp_all, v_t, v_s = _proj_call(hp, hs, w_in, l, cb_v, att_dim, rope_p, rope_s, tm, half=0,
                                      outs=("heads", "chunk_t"), outs_s=("f32",), tck=blk, cache=vp_all)
        qi_hm, qi_hm_s = _proj_call(hp, hs, w_in, l, cb_qi, idx_q, ropei_p, ropei_s, tm, half=D_IDX // 2,
                                    outs=("idx_heads",), outs_s=("idx_heads",))
        kw, ki16, wi_t, kw_s, ki16_s = _kidx_call(hp, hs, w_in, l, o_ki // LANES, ropei_p, ropei_s, tm)

        att = _dsa_call(q, qi_hm, wi_t, k16, v_t, ki16, topk_p, blk)
        qi_hb = qi_hm_s[:, :db].reshape(N_IDX_HEADS * db, D_IDX)
        w_hb = kw_s[:db, D_IDX:D_IDX + N_IDX_HEADS].T.reshape(N_IDX_HEADS * db, 1)
        scores = _pool_scores_call(qi_hb, w_hb, kidx_t, l, db, _pages_per_step(n_pool))
        rows = _select_call(page_table, scores.reshape(db, n_pool, PAGE_SIZE), qi_hb, w_hb, ki16_s[:db], topk_s)
        att_s = _gather_attend_call(rows.reshape(db, topk_s),
                                    q_s[:db].reshape(db, N_HEADS, HEAD_DIM),
                                    k_s[:db].reshape(db, N_HEADS, HEAD_DIM),
                                    v_s[:db].reshape(db, N_HEADS, HEAD_DIM),
                                    ck_rows, cv_rows, l, topk_s)
        att_s = jnp.zeros((rs, att_dim), BF16).at[:db].set(att_s.reshape(db, att_dim).astype(BF16))

        merged, merged_s = _merge_call(hp, a_in, att, hs, a_in_s, att_s, w_gates, w_pa, w_pb, l, tm, tn)
        xp, h2, xs, h2_s = _resid_norm_call(xp, merged, xs, merged_s, w_o16, l, g_ffn[l],
                                            gt1, sc2, sh2, gt1s, sc2s, sh2s, tm_full)
        act, act_s = _ffn_up_call(h2, h2_s, w_gate, w_up, l, tm, tn)
        xp, nxt, xs, nxt_s = _ffn_down_call(xp, act, xs, act_s, w_down16, l, g_next, gt2, mod_p[ln, :, :, 1],
                                            mod_p[ln, :, :, 0], gt2s, mod_s[ln, :, 1], mod_s[ln, :, 0],
                                            tm_full, tk_ffn, last)
        if last:
            yp, ys = nxt, nxt_s
        else:
            hp, hs = nxt, nxt_s
        outs["kip"].append(kw[:, :, :D_IDX].reshape(bp, tp // PAGE_SIZE, PAGE_SIZE, D_IDX))
        outs["cp"].append(conv_st)
        outs["ks"].append(k_s[:db].reshape(db, ts, N_HEADS, HEAD_DIM))
        outs["vs"].append(v_s[:db].reshape(db, ts, N_HEADS, HEAD_DIM))
        outs["kis"].append(kw_s[:db, :D_IDX].reshape(db, ts, D_IDX))
        outs["cs"].append(jnp.stack([prev_s[:db, 1], u_s[:db]], axis=1))

    page_shape = (depth, bp, tp // PAGE_SIZE, PAGE_SIZE, N_HEADS, HEAD_DIM)
    return (yp, ys[:db].reshape(db, ts, d),
            kp_all.reshape(page_shape), vp_all.reshape(page_shape), jnp.stack(outs["kip"]), jnp.stack(outs["cp"]),
            jnp.stack(outs["ks"]), jnp.stack(outs["vs"]), jnp.stack(outs["kis"]), jnp.stack(outs["cs"]))
```

```python
import functools

import jax
import jax.numpy as jnp
import numpy as np
from jax import lax
from jax.experimental import pallas as pl
from jax.experimental.pallas import tpu as pltpu

F32 = jnp.float32
BF16 = jnp.bfloat16
I32 = jnp.int32
I16 = jnp.int16

N_HEADS = 8
HEAD_DIM = 128
N_IDX_HEADS = 16
D_IDX = 64
CONV_W = 3
TOPK_MAX = 256
PAGE_SIZE = 128
ROPE_THETA = 10000.0
EPS = 1e-6
INDEX_SCALE = (D_IDX ** -0.5) * (N_IDX_HEADS ** -0.5)
ATT_SCALE = HEAD_DIM ** -0.5
EXP2_SCALE = ATT_SCALE * 1.4426950408889634

LANES = 128
SUBLANES = 8
VMEM_LIMIT = 56 * 1024 * 1024
INT_MIN = -2 ** 31
HALF16 = 2 ** 15
M_FLOOR = -1e30
ATT_BLOCK = 256
SAMPLE_ROWS = 16
ROW_TILE = 1024
ROW_TILE_FULL = 512
COL_TILE = 512


def _params(sem):
    return pltpu.CompilerParams(dimension_semantics=sem, vmem_limit_bytes=VMEM_LIMIT)


def _dot(a, b):
    return jnp.dot(a, b, preferred_element_type=F32)


def _dot_nt(a, b):
    return lax.dot_general(a, b, (((1,), (1,)), ((), ())), preferred_element_type=F32)


def _rmsnorm(x):
    return x * lax.rsqrt(jnp.mean(x * x, axis=-1, keepdims=True) + EPS)


def _sort_key(s):
    s = jnp.where(s == 0.0, 0.0, s)
    b = pltpu.bitcast(s, I32)
    return jnp.where(b < 0, b ^ 0x7FFFFFFF, b)


def _w_spec(k, tn, layer, col0=0, col_axis=None, row_axis=None):
    def index(*ids):
        return (layer, 0 if row_axis is None else ids[row_axis], col0 + (0 if col_axis is None else ids[col_axis]))
    return pl.BlockSpec((None, k, tn), index)


def _wt_spec(tn, k, layer, row0=0, row_axis=None):
    return pl.BlockSpec((None, tn, k), lambda *ids: (layer, row0 + (0 if row_axis is None else ids[row_axis]), 0))


def _wt_rows_spec(tn, k, layer, first_row, row_axis):
    assert first_row % SUBLANES == 0 and tn % SUBLANES == 0
    return pl.BlockSpec((pl.Element(1), pl.Element(tn), pl.Element(k)),
                        lambda *ids: (layer, pl.multiple_of(first_row + tn * ids[row_axis], SUBLANES), 0))


def _whole(a):
    return pl.BlockSpec(a.shape, lambda *ids: (0,) * a.ndim)


def _batch_vec(d, batch_axis):
    return pl.BlockSpec((None, 1, d), lambda *ids: (ids[batch_axis], 0, 0))


def _mod_kernel(c_ref, w_ref, b_ref, o_ref):
    c = c_ref[...]
    a = (c * jax.nn.sigmoid(c)).astype(BF16)
    o_ref[...] = _dot(a, w_ref[...].astype(BF16)) + b_ref[...]


def _mod_call(c_all, w_mod, b_mod):
    depth, d, n = w_mod.shape
    rows = c_all.shape[0]
    tn = 1024
    return pl.pallas_call(
        _mod_kernel,
        grid=(depth, n // tn),
        in_specs=[pl.BlockSpec((rows, d), lambda l, j: (0, 0)),
                  pl.BlockSpec((None, d, tn), lambda l, j: (l, 0, j)),
                  pl.BlockSpec((None, 1, tn), lambda l, j: (l, 0, j))],
        out_specs=pl.BlockSpec((None, rows, tn), lambda l, j: (l, 0, j)),
        out_shape=jax.ShapeDtypeStruct((depth, rows, n), F32),
        compiler_params=_params(("arbitrary", "arbitrary")),
        name="mod",
    )(c_all, w_mod, b_mod.reshape(depth, 1, n))


def _norm_mod(x, g, sc, sh):
    return (_rmsnorm(x) * g * (1.0 + sc) + sh).astype(BF16)


def _norm_mod_kernel(x_ref, xs_ref, g_ref, sc_ref, sh_ref, scs_ref, shs_ref, o_ref, os_ref):
    @pl.when((pl.program_id(0) == 0) & (pl.program_id(1) == 0))
    def _():
        os_ref[...] = _norm_mod(xs_ref[...], g_ref[...], scs_ref[...], shs_ref[...])

    o_ref[...] = _norm_mod(x_ref[...], g_ref[...], sc_ref[...], sh_ref[...])


def _norm_mod_call(x, xs, g, sc, sh, scs, shs, tm):
    gn, r, d = x.shape
    row = pl.BlockSpec((None, tm, d), lambda b, i: (b, i, 0))
    g = g.reshape(1, d)
    return pl.pallas_call(
        _norm_mod_kernel,
        grid=(gn, r // tm),
        in_specs=[row, _whole(xs), _whole(g), _batch_vec(d, 0), _batch_vec(d, 0), _whole(scs), _whole(shs)],
        out_specs=[row, _whole(xs)],
        out_shape=[jax.ShapeDtypeStruct((gn, r, d), BF16), jax.ShapeDtypeStruct(xs.shape, BF16)],
        compiler_params=_params(("arbitrary", "arbitrary")),
        name="norm_mod",
    )(x, xs, g, sc, sh, scs, shs)


def _conv_kernel(h_ref, hs_ref, wb_ref, wc_ref, wv_ref, cw_ref, prev_ref, p0_ref, p1_ref,
                 a_ref, st_ref, as_ref, us_ref, buf, w16, *, tm):
    b, i = pl.program_id(1), pl.program_id(2)
    cw = cw_ref[...]

    @pl.when((b == 0) & (i == 0))
    def _():
        for k, w_ref in enumerate((wb_ref, wc_ref, wv_ref)):
            w16[k] = w_ref[...].T.astype(BF16)
        hs = hs_ref[...]
        us = _dot(hs, w16[1]) * _dot(hs, w16[2])
        ys = cw[0:1] * p0_ref[...] + cw[1:2] * p1_ref[...] + cw[2:3] * us
        as_ref[...] = (_dot(hs, w16[0]) * ys).astype(BF16)
        us_ref[...] = us

    h = h_ref[...]
    bg = _dot(h, w16[0])
    u = _dot(h, w16[1]) * _dot(h, w16[2])

    @pl.when(i == 0)
    def _():
        buf[6:8, :] = prev_ref[...]

    buf[8:8 + tm, :] = u
    y = cw[0:1] * buf[6:6 + tm, :] + cw[1:2] * buf[7:7 + tm, :] + cw[2:3] * u
    a_ref[...] = (bg * y).astype(BF16)
    last2 = buf[6 + tm:8 + tm, :]
    buf[6:8, :] = last2
    st_ref[...] = last2


def _conv_call(h, hs, w_in, layer, cw, prev, p0, p1, tm, tn):
    gn, r, d = h.shape
    rs = hs.shape[0]
    n = cw.shape[1]
    nj = n // tn
    s_tile = pl.BlockSpec((rs, tn), lambda j, b, i: (0, j))
    return pl.pallas_call(
        functools.partial(_conv_kernel, tm=tm),
        grid=(nj, gn, r // tm),
        in_specs=[pl.BlockSpec((None, tm, d), lambda j, b, i: (b, i, 0)), _whole(hs),
                  _wt_spec(tn, d, layer, 0, 0), _wt_spec(tn, d, layer, nj, 0), _wt_spec(tn, d, layer, 2 * nj, 0),
                  pl.BlockSpec((CONV_W, tn), lambda j, b, i: (0, j)),
                  pl.BlockSpec((None, CONV_W - 1, tn), lambda j, b, i: (b, 0, j)),
                  s_tile, s_tile],
        out_specs=[pl.BlockSpec((None, tm, tn), lambda j, b, i: (b, i, j)),
                   pl.BlockSpec((None, CONV_W - 1, tn), lambda j, b, i: (b, 0, j)),
                   s_tile, s_tile],
        out_shape=[jax.ShapeDtypeStruct((gn, r, n), BF16), jax.ShapeDtypeStruct((gn, CONV_W - 1, n), F32),
                   jax.ShapeDtypeStruct((rs, n), BF16), jax.ShapeDtypeStruct((rs, n), F32)],
        scratch_shapes=[pltpu.VMEM((8 + tm, tn), F32), pltpu.VMEM((3, d, tn), BF16)],
        compiler_params=_params(("arbitrary", "arbitrary", "arbitrary")),
        name="conv",
    )(h, hs, w_in, w_in, w_in, cw, prev, p0, p1)


def _rope_group(x, cos, sin, half):
    if 2 * half == LANES:
        rot = pltpu.roll(x, half, 1)
    else:
        lane = lax.broadcasted_iota(I32, x.shape, 1)
        rot = jnp.where(lane % (2 * half) < half, pltpu.roll(x, LANES - half, 1), pltpu.roll(x, half, 1))
    return x * cos + rot * sin


def _proj_emit(acc, cos, sin, half, kinds, refs, tck):
    groups = []
    for c in range(acc.shape[1] // LANES):
        x = acc[:, c * LANES:(c + 1) * LANES]
        groups.append(_rope_group(x, cos, sin, half) if half else x)
    for kind, o in zip(kinds, refs):
        for c, x in enumerate(groups):
            if kind == "f32":
                o[:, c * LANES:(c + 1) * LANES] = x
            elif kind == "bf16":
                o[:, c * LANES:(c + 1) * LANES] = x.astype(BF16)
            elif kind == "heads":
                o[:, c, :] = x
            elif kind == "idx_heads":
                xb = x.astype(BF16)
                o[2 * c] = xb[:, :D_IDX]
                o[2 * c + 1] = xb[:, D_IDX:]
            elif kind == "chunk_t":
                for k in range(x.shape[0] // tck):
                    o[k, c * LANES:(c + 1) * LANES, :] = x[k * tck:(k + 1) * tck, :].T.astype(BF16)


def _proj_kernel(*refs, half, outs, outs_s, tck, aliased):
    h_ref, hs_ref, w_ref, cos_ref, sin_ref, cos_s_ref, sin_s_ref = refs[:7]
    rest = refs[7 + aliased:]
    out_refs, out_s_refs, w16 = rest[:len(outs)], rest[len(outs):len(outs) + len(outs_s)], rest[-1]

    @pl.when((pl.program_id(0) == 0) & (pl.program_id(1) == 0))
    def _():
        w16[...] = w_ref[...].T.astype(BF16)
        _proj_emit(_dot(hs_ref[...], w16[...]), cos_s_ref[...], sin_s_ref[...], half, outs_s, out_s_refs, None)

    _proj_emit(_dot(h_ref[...], w16[...]), cos_ref[...], sin_ref[...], half, outs, out_refs, tck)


def _proj_call(h, hs, w_in, layer, col0, n, rope, rope_s, tm, *, half, outs, outs_s, tck=None, cache=None):
    gn, r, d = h.shape
    rs = hs.shape[0]
    heads = n // LANES
    depth = None if cache is None else cache.shape[0]
    out_specs, out_shape = [], []
    for kind in outs:
        if kind == "bf16":
            out_specs.append(pl.BlockSpec((None, tm, n), lambda b, i: (b, i, 0)))
            out_shape.append(jax.ShapeDtypeStruct((gn, r, n), BF16))
        elif kind == "heads":
            out_specs.append(pl.BlockSpec((None, None, tm, heads, LANES), lambda b, i: (layer, b, i, 0, 0)))
            out_shape.append(jax.ShapeDtypeStruct((depth, gn, r, heads, LANES), F32))
        elif kind == "idx_heads":
            out_specs.append(pl.BlockSpec((None, n // D_IDX, tm, D_IDX), lambda b, i: (b, 0, i, 0)))
            out_shape.append(jax.ShapeDtypeStruct((gn, n // D_IDX, r, D_IDX), BF16))
        elif kind == "chunk_t":
            out_specs.append(pl.BlockSpec((None, tm // tck, n, tck), lambda b, i: (b, i, 0, 0)))
            out_shape.append(jax.ShapeDtypeStruct((gn, r // tck, n, tck), BF16))
    for kind in outs_s:
        shape, dtype = ((rs, n), F32) if kind == "f32" else ((n // D_IDX, rs, D_IDX), BF16)
        out_specs.append(pl.BlockSpec(shape, lambda b, i, nd=len(shape): (0,) * nd))
        out_shape.append(jax.ShapeDtypeStruct(shape, dtype))
    tab = pl.BlockSpec((tm, LANES), lambda b, i: (i, 0))
    in_specs = [pl.BlockSpec((None, tm, d), lambda b, i: (b, i, 0)), _whole(hs), _wt_spec(n, d, layer, col0),
                tab, tab, _whole(rope_s[0]), _whole(rope_s[1])]
    args = [h, hs, w_in, rope[0], rope[1], rope_s[0], rope_s[1]]
    aliases = {}
    if cache is not None:
        in_specs.append(pl.BlockSpec(memory_space=pl.ANY))
        args.append(cache)
        aliases = {7: outs.index("heads")}
    return pl.pallas_call(
        functools.partial(_proj_kernel, half=half, outs=outs, outs_s=outs_s, tck=tck, aliased=cache is not None),
        grid=(gn, r // tm),
        in_specs=in_specs, out_specs=out_specs, out_shape=out_shape,
        scratch_shapes=[pltpu.VMEM((d, n), BF16)],
        input_output_aliases=aliases,
        compiler_params=_params(("arbitrary", "arbitrary")),
        name="proj",
    )(*args)


def _kidx_rows(acc, cos, sin):
    roped = _rope_group(acc, cos, sin, D_IDX // 2)
    lane = lax.broadcasted_iota(I32, acc.shape, 1)
    return jnp.where(lane < D_IDX, roped, jnp.where(lane < D_IDX + N_IDX_HEADS, acc, 0.0))


def _kidx_kernel(h_ref, hs_ref, w_ref, cos_ref, sin_ref, cos_s_ref, sin_s_ref,
                 kw_ref, kb_ref, wt_ref, kws_ref, kbs_ref, w16):
    @pl.when((pl.program_id(0) == 0) & (pl.program_id(1) == 0))
    def _():
        w16[...] = w_ref[...].T.astype(BF16)
        kws = _kidx_rows(_dot(hs_ref[...], w16[...]), cos_s_ref[...], sin_s_ref[...])
        kws_ref[...] = kws
        kbs_ref[...] = kws[:, :D_IDX].astype(BF16)

    kw = _kidx_rows(_dot(h_ref[...], w16[...]), cos_ref[...], sin_ref[...])
    kw_ref[...] = kw
    kb_ref[...] = kw[:, :D_IDX].astype(BF16)
    wt_ref[...] = kw.T[D_IDX:D_IDX + N_IDX_HEADS, :]


def _kidx_call(h, hs, w_in, layer, col0, rope, rope_s, tm):
    gn, r, d = h.shape
    rs = hs.shape[0]
    tab = pl.BlockSpec((tm, LANES), lambda b, i: (i, 0))
    s128 = pl.BlockSpec((rs, LANES), lambda b, i: (0, 0))
    s64 = pl.BlockSpec((rs, D_IDX), lambda b, i: (0, 0))
    return pl.pallas_call(
        _kidx_kernel,
        grid=(gn, r // tm),
        in_specs=[pl.BlockSpec((None, tm, d), lambda b, i: (b, i, 0)), _whole(hs), _wt_spec(LANES, d, layer, col0),
                  tab, tab, s128, s128],
        out_specs=[pl.BlockSpec((None, tm, LANES), lambda b, i: (b, i, 0)),
                   pl.BlockSpec((None, tm, D_IDX), lambda b, i: (b, i, 0)),
                   pl.BlockSpec((None, N_IDX_HEADS, tm), lambda b, i: (b, 0, i)),
                   s128, s64],
        out_shape=[jax.ShapeDtypeStruct((gn, r, LANES), F32), jax.ShapeDtypeStruct((gn, r, D_IDX), BF16),
                   jax.ShapeDtypeStruct((gn, N_IDX_HEADS, r), F32),
                   jax.ShapeDtypeStruct((rs, LANES), F32), jax.ShapeDtypeStruct((rs, D_IDX), BF16)],
        scratch_shapes=[pltpu.VMEM((d, LANES), BF16)],
        compiler_params=_params(("arbitrary", "arbitrary")),
        name="kidx",
    )(h, hs, w_in, rope[0], rope[1], rope_s[0], rope_s[1])


def _strict_tri(n, dtype, lower):
    row = lax.broadcasted_iota(I32, (n, n), 0)
    col = lax.broadcasted_iota(I32, (n, n), 1)
    return jnp.where(col < row if lower else row < col, 1.0, 0.0).astype(dtype)


def _dsa_kernel(q_ref, qi_ref, wi_ref, k_ref, vt_ref, ki_ref, o_ref,
                key_scr, hi_scr, lo_scr, m_scr, l_scr, acc_scr, *, topk, ck, qb):
    i = pl.program_id(1)
    nch = ((i + 1) * qb + ck - 1) // ck
    qpos = i * qb + lax.broadcasted_iota(I32, (LANES, qb), 1)
    krow = lax.broadcasted_iota(I32, (LANES, qb), 0)
    w = wi_ref[...]

    def index_chunk(c, _):
        base = pl.multiple_of(c * ck, ck)
        for r in range(ck // LANES):
            rows = slice(r * LANES, (r + 1) * LANES)
            kc = ki_ref[pl.ds(base + r * LANES, LANES), :]
            acc = jnp.zeros((LANES, qb), F32)
            for h in range(N_IDX_HEADS):
                acc = acc + jnp.maximum(_dot_nt(kc, qi_ref[h]), 0.0) * w[h:h + 1, :]
            kpos = base + r * LANES + krow
            key = jnp.where(kpos <= qpos, _sort_key(acc * INDEX_SCALE), INT_MIN)
            key_scr[c, rows, :] = key
            hi_scr[c, rows, :] = lax.shift_right_arithmetic(key, 16).astype(I16)
            lo_scr[c, rows, :] = ((key & 0xFFFF) - HALF16).astype(I16)
        return 0

    lax.fori_loop(0, nch, index_chunk, 0)

    ge = lambda a, b: a >= b
    gt = lambda a, b: a > b
    pack = 2 * SUBLANES

    def count16(scr, pred, thr):
        thr = thr.astype(I16)

        def body(c, part):
            hit = jnp.where(pred(scr[c], thr), jnp.ones((), BF16), jnp.zeros((), BF16))
            terms = [hit[k * pack:(k + 1) * pack] for k in range(ck // pack)]
            while len(terms) > 1:
                terms = [a + b for a, b in zip(terms[0::2], terms[1::2])]
            return part + terms[0]

        part = lax.fori_loop(0, nch, body, jnp.zeros((pack, qb), BF16))
        return jnp.sum(part.astype(F32), axis=0, keepdims=True).astype(I32)

    def search16(scr, need):
        def step(it, t):
            cand = t + jnp.left_shift(jnp.int32(1), 15 - it)
            return jnp.where(count16(scr, ge, cand) >= need, cand, t)
        return lax.fori_loop(0, 16, step, jnp.full((1, qb), -HALF16, I32))

    t_hi = search16(hi_scr, topk)
    need_lo = topk - count16(hi_scr, gt, t_hi)
    t_hi16 = t_hi.astype(I16)

    def keep_band(c, _):
        lo_scr[c] = jnp.where(hi_scr[c] == t_hi16, lo_scr[c], jnp.full((), -HALF16, I16))
        return 0

    lax.fori_loop(0, nch, keep_band, 0)
    t_lo = search16(lo_scr, need_lo)
    t = t_hi * (2 * HALF16) + (t_lo + HALF16)
    t = jnp.maximum(t, INT_MIN + 1)

    def count(pred, thr):
        def body(c, cnt):
            hit = jnp.where(pred(key_scr[c], thr), 1, 0)
            return cnt + jnp.sum(hit.reshape(ck // SUBLANES, SUBLANES, qb), axis=0)
        cnt = lax.fori_loop(0, nch, body, jnp.zeros((SUBLANES, qb), I32))
        return jnp.sum(cnt, axis=0, keepdims=True)

    c_ge = count(ge, t)

    @pl.when(jnp.max(c_ge) > topk)
    def _():
        need = (topk - count(gt, t)).astype(F32)
        lower = _strict_tri(ck, BF16, lower=True)

        def body(c, carry):
            kc = key_scr[c]
            eq = kc == t
            eqf = jnp.where(eq, 1.0, 0.0)
            before = _dot(lower, eqf.astype(BF16)) + carry
            key_scr[c] = jnp.where(eq & (before >= need), INT_MIN, kc)
            return carry + jnp.sum(eqf, axis=0, keepdims=True)

        lax.fori_loop(0, nch, body, jnp.zeros((1, qb), F32))

    m_scr[...] = jnp.full(m_scr.shape, M_FLOOR, F32)
    l_scr[...] = jnp.zeros(l_scr.shape, F32)
    acc_scr[...] = jnp.zeros(acc_scr.shape, F32)

    def attend_chunk(c, _):
        base = pl.multiple_of(c * ck, ck)
        sel = key_scr[c] >= t
        heads = [slice(h * HEAD_DIM, (h + 1) * HEAD_DIM) for h in range(N_HEADS)]
        raw = [_dot_nt(k_ref[pl.ds(base, ck), cols], q_ref[:, cols]) for cols in heads]
        probs = []
        for h in range(N_HEADS):
            s = jnp.where(sel, raw[h], -jnp.inf)
            m_old = m_scr[h]
            m_new = jnp.maximum(m_old, jnp.max(s, axis=0, keepdims=True))
            alpha = jnp.exp2((m_old - m_new) * EXP2_SCALE)
            p = jnp.exp2((s - m_new) * EXP2_SCALE)
            l_scr[h] = alpha * l_scr[h] + jnp.sum(p, axis=0, keepdims=True)
            m_scr[h] = m_new
            probs.append((alpha, p.astype(BF16)))
        for h in range(N_HEADS):
            alpha, p = probs[h]
            acc_scr[h] = alpha * acc_scr[h] + _dot(vt_ref[c, heads[h], :], p)
        return 0

    lax.fori_loop(0, nch, attend_chunk, 0)
    for h in range(N_HEADS):
        o_ref[:, h * HEAD_DIM:(h + 1) * HEAD_DIM] = (acc_scr[h] / l_scr[h]).T.astype(o_ref.dtype)


def _dsa_call(q, qi_hm, wi_t, k, v_t, ki, topk, blk):
    bn, t, a = q.shape
    assert t <= 256 * 2 * SUBLANES, "the threshold search counts in bf16: at most 256 keys per packed-row entry"
    full = lambda shape: pl.BlockSpec((None,) + shape, lambda b, i: (b,) + (0,) * len(shape))
    return pl.pallas_call(
        functools.partial(_dsa_kernel, topk=topk, ck=blk, qb=blk),
        grid=(bn, t // blk),
        in_specs=[pl.BlockSpec((None, blk, a), lambda b, i: (b, i, 0)),
                  pl.BlockSpec((None, N_IDX_HEADS, blk, D_IDX), lambda b, i: (b, 0, i, 0)),
                  pl.BlockSpec((None, N_IDX_HEADS, blk), lambda b, i: (b, 0, i)),
                  full((t, a)), full((t // blk, a, blk)), full((t, D_IDX))],
        out_specs=pl.BlockSpec((None, blk, a), lambda b, i: (b, i, 0)),
        out_shape=jax.ShapeDtypeStruct((bn, t, a), BF16),
        scratch_shapes=[pltpu.VMEM((t // blk, blk, blk), I32),
                        pltpu.VMEM((t // blk, blk, blk), I16),
                        pltpu.VMEM((t // blk, blk, blk), I16),
                        pltpu.VMEM((N_HEADS, 1, blk), F32),
                        pltpu.VMEM((N_HEADS, 1, blk), F32),
                        pltpu.VMEM((N_HEADS, HEAD_DIM, blk), F32)],
        compiler_params=_params(("arbitrary", "arbitrary")),
        name="dsa_prompt",
    )(q, qi_hm, wi_t, k, v_t, ki)


def _pool_scores_kernel(qi_ref, w_ref, ck_ref, o_ref, *, nb, group):
    pages, _, page = ck_ref.shape
    qi, w = qi_ref[...], w_ref[...]
    for p in range(0, pages, group):
        kc = jnp.concatenate([ck_ref[p + g] for g in range(group)], axis=1).astype(BF16)
        d = jnp.maximum(_dot(qi, kc), 0.0) * w
        s = d[0:nb]
        for h in range(1, N_IDX_HEADS):
            s = s + d[h * nb:(h + 1) * nb]
        o_ref[:, p * page:(p + group) * page] = s * INDEX_SCALE


def _pool_scores_call(qi_hb, w_hb, kidx_t, layer, nb, pages_per_step):
    depth, n_pool, di, page = kidx_t.shape
    group = 2 if pages_per_step % 2 == 0 else 1
    return pl.pallas_call(
        functools.partial(_pool_scores_kernel, nb=nb, group=group),
        grid=(n_pool // pages_per_step,),
        in_specs=[pl.BlockSpec(qi_hb.shape, lambda j: (0, 0)),
                  pl.BlockSpec(w_hb.shape, lambda j: (0, 0)),
                  pl.BlockSpec((None, pages_per_step, di, page), lambda j: (layer, j, 0, 0))],
        out_specs=pl.BlockSpec((nb, pages_per_step * page), lambda j: (0, j)),
        out_shape=jax.ShapeDtypeStruct((nb, n_pool * page), F32),
        compiler_params=_params(("arbitrary",)),
        name="pool_scores",
    )(qi_hb, w_hb, kidx_t)


def _select_kernel(pt_ref, sc_ref, ptv_ref, qi_ref, w_ref, kn_ref, row_ref, s_scr, *, topk, n_pages, nb):
    def gather_page(p, _):
        for b in range(nb):
            s_scr[b, pl.ds(p, 1), :] = sc_ref[b, pl.ds(pt_ref[b, p], 1), :]
        return 0

    lax.fori_loop(0, n_pages, gather_page, 0)
    key = _sort_key(s_scr[...])

    def total(x):
        return jnp.sum(jnp.sum(x, axis=1, keepdims=True), axis=2, keepdims=True)

    d = jnp.maximum(_dot_nt(qi_ref[...], kn_ref[...]), 0.0) * w_ref[...]
    row = lax.broadcasted_iota(I32, d.shape, 0)
    col = lax.broadcasted_iota(I32, d.shape, 1)
    s_new = jnp.stack([jnp.sum(jnp.where((row % nb == b) & (col == b), d, 0.0), keepdims=True) for b in range(nb)])
    key_new = _sort_key(s_new * INDEX_SCALE)

    def search(it, t):
        cand = t + jnp.left_shift(jnp.int32(1), 31 - it)
        n_ge = total(jnp.where(key >= cand, 1, 0)) + jnp.where(key_new >= cand, 1, 0)
        return jnp.where(n_ge >= topk, cand, t)

    t = lax.fori_loop(0, 32, search, jnp.full((nb, 1, 1), INT_MIN, I32))
    t = jnp.maximum(t, INT_MIN + 1)

    upper = _strict_tri(PAGE_SIZE, BF16, lower=False)
    lower = _strict_tri(n_pages, BF16, lower=True)
    ones = jnp.ones((SUBLANES, PAGE_SIZE), BF16)
    incl = jnp.where(lax.broadcasted_iota(I32, (n_pages, n_pages), 0) <= lax.broadcasted_iota(I32, (n_pages, n_pages), 1),
                     1.0, 0.0).astype(BF16)
    slot = lax.broadcasted_iota(I32, (topk, PAGE_SIZE), 0).astype(F32)
    lane = lax.broadcasted_iota(I32, (topk, PAGE_SIZE), 1)
    page_lane = lax.broadcasted_iota(I32, (topk, n_pages), 1)

    def ranks(f16):
        per_page = jnp.broadcast_to(jnp.sum(f16.astype(F32), axis=1, keepdims=True), f16.shape)
        return _dot(f16, upper), _dot(lower, per_page.astype(BF16))

    for b in range(nb):
        kb, tb, knb = key[b], t[b], key_new[b]
        n_gt = jnp.sum(jnp.where(kb > tb, 1.0, 0.0), keepdims=True) + jnp.where(knb > tb, 1.0, 0.0)
        need = topk - n_gt
        eq = kb == tb
        eq16 = jnp.where(eq, 1.0, 0.0).astype(BF16)
        eq_in, eq_before = ranks(eq16)
        sel = (kb > tb) | (eq & (eq_in + eq_before < need))
        sel_new = (knb > tb) | ((knb == tb) & (jnp.sum(eq16.astype(F32), keepdims=True) < need))

        sel16 = jnp.where(sel, 1.0, 0.0).astype(BF16)
        in_page, before_page = ranks(sel16)
        n_sel = jnp.sum(sel16.astype(F32), keepdims=True)
        cum = _dot(_dot_nt(ones, sel16).astype(BF16), incl)[0:1, :]
        page_r = jnp.sum(jnp.where(cum <= slot[:, 0:1], 1.0, 0.0), axis=1, keepdims=True).astype(I32)
        hit = page_lane == page_r
        onehot = jnp.where(hit, 1.0, 0.0).astype(BF16)
        got = _dot(onehot, jnp.where(sel, in_page, -1.0).astype(BF16))
        within = slot - _dot(onehot, before_page.astype(BF16))
        off = jnp.sum(jnp.where(got == within, lane, 0), axis=1, keepdims=True)
        phys = jnp.sum(jnp.where(hit, ptv_ref[b:b + 1, :], 0), axis=1, keepdims=True)
        rows = jnp.where(slot[:, 0:1] < n_sel, phys * PAGE_SIZE + off, 0)
        row_ref[b] = rows - jnp.where(sel_new & (slot[:, 0:1] == n_sel), 1, 0)


def _select_call(page_table, scores, qi_hb, w_hb, k_new, topk):
    nb, n_pages = page_table.shape
    whole = lambda a: pl.BlockSpec(a.shape, lambda i, pt: (0,) * a.ndim)
    return pl.pallas_call(
        functools.partial(_select_kernel, topk=topk, n_pages=n_pages, nb=nb),
        grid_spec=pltpu.PrefetchScalarGridSpec(
            num_scalar_prefetch=1, grid=(1,),
            in_specs=[whole(scores), whole(page_table), whole(qi_hb), whole(w_hb), whole(k_new)],
            out_specs=pl.BlockSpec((nb, topk, 1), lambda i, pt: (0, 0, 0)),
            scratch_shapes=[pltpu.VMEM((nb, n_pages, PAGE_SIZE), F32)]),
        out_shape=jax.ShapeDtypeStruct((nb, topk, 1), I32),
        compiler_params=_params(("arbitrary",)),
        name="select",
    )(page_table, scores, page_table, qi_hb, w_hb, k_new)


def _gather_attend_kernel(row_ref, q_ref, kn_ref, vn_ref, ck_hbm, cv_hbm, o_ref, kbuf, vbuf, sem, *, layer, topk):
    b = pl.program_id(0)

    def copies(r):
        src = jnp.maximum(row_ref[b, r], 0)
        return (pltpu.make_async_copy(ck_hbm.at[layer, src], kbuf.at[r], sem.at[0]),
                pltpu.make_async_copy(cv_hbm.at[layer, src], vbuf.at[r], sem.at[1]))

    def start(r, _):
        for cp in copies(r):
            cp.start()
        return 0

    def wait(r, _):
        for cp in copies(r):
            cp.wait()
        return 0

    lax.fori_loop(0, topk, start, 0)
    lax.fori_loop(0, topk, wait, 0)

    @pl.when(row_ref[b, topk - 1] < 0)
    def _():
        kbuf[topk - 1] = kn_ref[...]
        vbuf[topk - 1] = vn_ref[...]

    q = q_ref[...]
    s = jnp.sum(kbuf[...] * q[None], axis=-1, keepdims=True) * ATT_SCALE
    p = jnp.exp(s - jnp.max(s, axis=0, keepdims=True))
    p = p / jnp.sum(p, axis=0, keepdims=True)
    o_ref[...] = jnp.sum(p * vbuf[...], axis=0)


def _gather_attend_call(rows, q, k_new, v_new, cache_k, cache_v, layer, topk):
    nb = rows.shape[0]
    any_spec = pl.BlockSpec(memory_space=pl.ANY)
    per_sample = pl.BlockSpec((None, N_HEADS, HEAD_DIM), lambda b, rows: (b, 0, 0))
    return pl.pallas_call(
        functools.partial(_gather_attend_kernel, layer=layer, topk=topk),
        grid_spec=pltpu.PrefetchScalarGridSpec(
            num_scalar_prefetch=1, grid=(nb,),
            in_specs=[per_sample, per_sample, per_sample, any_spec, any_spec],
            out_specs=per_sample,
            scratch_shapes=[pltpu.VMEM((topk, N_HEADS, HEAD_DIM), F32),
                            pltpu.VMEM((topk, N_HEADS, HEAD_DIM), F32),
                            pltpu.SemaphoreType.DMA((2,))]),
        out_shape=jax.ShapeDtypeStruct((nb, N_HEADS, HEAD_DIM), F32),
        compiler_params=_params(("arbitrary",)),
        name="gather_attend",
    )(rows, q, k_new, v_new, cache_k, cache_v)


def _merge_kernel(h_ref, a_ref, t_ref, hs_ref, as_ref, ts_ref, wga_ref, wgb_ref, wpa_ref, wpb_ref,
                  o_ref, os_ref, wga16, wgb16, wpa16, wpb16):
    def mix(h, a, t):
        ya = _dot(a, wpa16[...])
        yb = _dot(t, wpb16[...])
        ga = jax.nn.sigmoid(_dot(h, wga16[...]))
        gb = jax.nn.sigmoid(_dot(h, wgb16[...]))
        return (ga * ya + gb * yb).astype(BF16)

    @pl.when((pl.program_id(1) == 0) & (pl.program_id(2) == 0))
    def _():
        wga16[...] = wga_ref[0].T.astype(BF16)
        wgb16[...] = wgb_ref[0].T.astype(BF16)
        wpa16[...] = wpa_ref[...].astype(BF16)
        wpb16[...] = wpb_ref[...].astype(BF16)
        os_ref[...] = mix(hs_ref[...], as_ref[...], ts_ref[...])

    o_ref[...] = mix(h_ref[...], a_ref[...], t_ref[...])


def _merge_call(h, a_in, att, hs, as_in, att_s, w_in, gate_row, w_pa, w_pb, layer, tm, tn):
    gn, r, d = h.shape
    a = a_in.shape[2]
    rs = hs.shape[0]
    nj = d // tn
    rows = lambda w: pl.BlockSpec((None, tm, w), lambda j, b, i: (b, i, 0))
    s_out = pl.BlockSpec((rs, tn), lambda j, b, i: (0, j))
    return pl.pallas_call(
        _merge_kernel,
        grid=(nj, gn, r // tm),
        in_specs=[rows(d), rows(a), rows(a), _whole(hs), _whole(as_in), _whole(att_s),
                  _wt_rows_spec(tn, d, layer, gate_row, 0), _wt_rows_spec(tn, d, layer, gate_row + d, 0),
                  _w_spec(a, tn, layer, 0, 0), _w_spec(a, tn, layer, 0, 0)],
        out_specs=[pl.BlockSpec((None, tm, tn), lambda j, b, i: (b, i, j)), s_out],
        out_shape=[jax.ShapeDtypeStruct((gn, r, d), BF16), jax.ShapeDtypeStruct((rs, d), BF16)],
        scratch_shapes=[pltpu.VMEM((d, tn), BF16), pltpu.VMEM((d, tn), BF16),
                        pltpu.VMEM((a, tn), BF16), pltpu.VMEM((a, tn), BF16)],
        compiler_params=_params(("arbitrary", "arbitrary", "arbitrary")),
        name="merge",
    )(h, a_in, att, hs, as_in, att_s, w_in, w_in, w_pa, w_pb)


def _resid_norm_kernel(x_ref, m_ref, xs_ref, ms_ref, w_ref, g_ref, gt_ref, sc_ref, sh_ref, gts_ref, scs_ref, shs_ref,
                       x_out, h_out, xs_out, hs_out):
    def step(x, m, gt, sc, sh):
        x = x + gt * _dot(m, w_ref[...])
        return x, _norm_mod(x, g_ref[...], sc, sh)

    @pl.when((pl.program_id(0) == 0) & (pl.program_id(1) == 0))
    def _():
        xs_out[...], hs_out[...] = step(xs_ref[...], ms_ref[...], gts_ref[...], scs_ref[...], shs_ref[...])

    x_out[...], h_out[...] = step(x_ref[...], m_ref[...], gt_ref[...], sc_ref[...], sh_ref[...])


def _resid_norm_call(x, m, xs, ms, w_o, layer, g, gt, sc, sh, gts, scs, shs, tm):
    gn, r, d = x.shape
    row = pl.BlockSpec((None, tm, d), lambda b, i: (b, i, 0))
    g = g.reshape(1, d)
    return pl.pallas_call(
        _resid_norm_kernel,
        grid=(gn, r // tm),
        in_specs=[row, row, _whole(xs), _whole(ms), _w_spec(d, d, layer), _whole(g),
                  _batch_vec(d, 0), _batch_vec(d, 0), _batch_vec(d, 0), _whole(gts), _whole(scs), _whole(shs)],
        out_specs=[row, row, _whole(xs), _whole(xs)],
        out_shape=[jax.ShapeDtypeStruct((gn, r, d), F32), jax.ShapeDtypeStruct((gn, r, d), BF16),
                   jax.ShapeDtypeStruct(xs.shape, F32), jax.ShapeDtypeStruct(xs.shape, BF16)],
        compiler_params=_params(("arbitrary", "arbitrary")),
        name="resid_norm",
    )(x, m, xs, ms, w_o, g, gt, sc, sh, gts, scs, shs)


def _ffn_up_kernel(h_ref, hs_ref, wg_ref, wu_ref, o_ref, os_ref, wg16, wu16):
    def act(h):
        g = _dot(h, wg16[...])
        return (g * jax.nn.sigmoid(g) * _dot(h, wu16[...])).astype(BF16)

    @pl.when((pl.program_id(1) == 0) & (pl.program_id(2) == 0))
    def _():
        wg16[...] = wg_ref[...].astype(BF16)
        wu16[...] = wu_ref[...].astype(BF16)
        os_ref[...] = act(hs_ref[...])

    o_ref[...] = act(h_ref[...])


def _ffn_up_call(h, hs, w_gate, w_up, layer, tm, tn):
    gn, r, d = h.shape
    rs = hs.shape[0]
    f = w_gate.shape[2]
    return pl.pallas_call(
        _ffn_up_kernel,
        grid=(f // tn, gn, r // tm),
        in_specs=[pl.BlockSpec((None, tm, d), lambda j, b, i: (b, i, 0)), _whole(hs),
                  _w_spec(d, tn, layer, 0, 0), _w_spec(d, tn, layer, 0, 0)],
        out_specs=[pl.BlockSpec((None, tm, tn), lambda j, b, i: (b, i, j)),
                   pl.BlockSpec((rs, tn), lambda j, b, i: (0, j))],
        out_shape=[jax.ShapeDtypeStruct((gn, r, f), BF16), jax.ShapeDtypeStruct((rs, f), BF16)],
        scratch_shapes=[pltpu.VMEM((d, tn), BF16), pltpu.VMEM((d, tn), BF16)],
        compiler_params=_params(("arbitrary", "arbitrary", "arbitrary")),
        name="ffn_up",
    )(h, hs, w_gate, w_up)


def _ffn_down_kernel(x_ref, a_ref, xs_ref, as_ref, w_ref, g_ref, gt_ref, sc_ref, sh_ref, gts_ref, scs_ref, shs_ref,
                     x_out, n_out, xs_out, ns_out, acc, acc_s, *, final):
    k = pl.program_id(2)
    last = k == pl.num_programs(2) - 1
    with_samples = (pl.program_id(0) == 0) & (pl.program_id(1) == 0)

    def finish(x, gt, sc, sh, total, x_o, n_o):
        x = x + gt * total
        x_o[...] = x
        n_o[...] = (_rmsnorm(x) * g_ref[...]).astype(n_o.dtype) if final else _norm_mod(x, g_ref[...], sc, sh)

    @pl.when(k == 0)
    def _():
        acc[...] = jnp.zeros(acc.shape, F32)

    acc[...] += _dot(a_ref[...], w_ref[...])

    @pl.when(with_samples & (k == 0))
    def _():
        acc_s[...] = jnp.zeros(acc_s.shape, F32)

    @pl.when(with_samples)
    def _():
        acc_s[...] += _dot(as_ref[...], w_ref[...])

    @pl.when(with_samples & last)
    def _():
        finish(xs_ref[...], gts_ref[...], scs_ref[...], shs_ref[...], acc_s[...], xs_out, ns_out)

    @pl.when(last)
    def _():
        finish(x_ref[...], gt_ref[...], sc_ref[...], sh_ref[...], acc[...], x_out, n_out)


def _ffn_down_call(x, act, xs, act_s, w_down, layer, g, gt, sc, sh, gts, scs, shs, tm, tk, final):
    gn, r, d = x.shape
    rs = xs.shape[0]
    f = act.shape[2]
    row = pl.BlockSpec((None, tm, d), lambda b, i, k: (b, i, 0))
    g = g.reshape(1, d)
    n_dtype = F32 if final else BF16
    return pl.pallas_call(
        functools.partial(_ffn_down_kernel, final=final),
        grid=(gn, r // tm, f // tk),
        in_specs=[row, pl.BlockSpec((None, tm, tk), lambda b, i, k: (b, i, k)),
                  _whole(xs), pl.BlockSpec((rs, tk), lambda b, i, k: (0, k)),
                  _w_spec(tk, d, layer, row_axis=2), _whole(g),
                  _batch_vec(d, 0), _batch_vec(d, 0), _batch_vec(d, 0), _whole(gts), _whole(scs), _whole(shs)],
        out_specs=[row, row, _whole(xs), _whole(xs)],
        out_shape=[jax.ShapeDtypeStruct((gn, r, d), F32), jax.ShapeDtypeStruct((gn, r, d), n_dtype),
                   jax.ShapeDtypeStruct(xs.shape, F32), jax.ShapeDtypeStruct(xs.shape, n_dtype)],
        scratch_shapes=[pltpu.VMEM((tm, d), F32), pltpu.VMEM((rs, d), F32)],
        compiler_params=_params(("arbitrary", "arbitrary", "arbitrary")),
        name="ffn_down",
    )(x, act, xs, act_s, w_down, g, gt, sc, sh, gts, scs, shs)


def _rope_tables(pos, d):
    inv_freq = ROPE_THETA ** (-np.arange(0, d, 2, dtype=np.float64) / d)
    ang = np.asarray(pos, np.float64)[:, None] * inv_freq[None, :]
    cos, sin = np.cos(ang), np.sin(ang)
    reps = LANES // d
    return (jnp.asarray(np.tile(np.concatenate([cos, cos], axis=1), (1, reps)), F32),
            jnp.asarray(np.tile(np.concatenate([-sin, sin], axis=1), (1, reps)), F32))


def _pages_per_step(n_pool):
    return max(p for p in range(1, 33) if n_pool % p == 0)


def kernel(x_prompt, x_sample, c_prompt, c_sample, cache_k, cache_v, cache_kidx, state_conv, page_table,
           w_mod, b_mod, g_mix, g_ffn, w_in, conv_w, w_pa, w_pb, w_o, w_gate, w_up, w_down, g_final):
    bp, tp, d = x_prompt.shape
    db, ts, _ = x_sample.shape
    assert ts == 1, "the sample path handles one new token per sequence"
    depth = w_mod.shape[0]
    n_pages = page_table.shape[1]
    past = n_pages * PAGE_SIZE
    att_dim = N_HEADS * HEAD_DIM
    d_conv = d // 2
    idx_q = N_IDX_HEADS * D_IDX
    topk_p = min(TOPK_MAX, tp // 4)
    topk_s = min(TOPK_MAX, (past + ts) // 4)
    rs = SAMPLE_ROWS
    blk = ATT_BLOCK if tp % ATT_BLOCK == 0 else LANES
    tm, tm_full, tn = min(tp, ROW_TILE), min(tp, ROW_TILE_FULL), COL_TILE
    assert db <= rs and tp % blk == 0 and tp % tm == 0 and d_conv == att_dim == idx_q

    cb_q, cb_k, cb_v, cb_qi = 3, 4, 5, 6
    o_ki = 7 * att_dim
    o_ga = o_ki + D_IDX + N_IDX_HEADS
    assert o_ki % LANES == 0

    n_pool = cache_kidx.shape[1]
    kidx_t = jnp.swapaxes(cache_kidx, 2, 3)
    ck_rows = cache_k.reshape(depth, n_pool * PAGE_SIZE, N_HEADS, HEAD_DIM)
    cv_rows = cache_v.reshape(depth, n_pool * PAGE_SIZE, N_HEADS, HEAD_DIM)

    w_in_t = jnp.swapaxes(w_in, 1, 2)
    w_o16, w_down16 = w_o.astype(BF16), w_down.astype(BF16)

    c_all = jnp.zeros((rs, d), F32).at[:bp].set(c_prompt).at[bp:bp + db].set(c_sample)
    mod = _mod_call(c_all, w_mod, b_mod)
    mod_p = mod[:, :bp].reshape(depth, bp, 1, 6, d)
    mod_s = jnp.zeros((depth, rs, 6, d), F32).at[:, :db].set(mod[:, bp:bp + db].reshape(depth, db, 6, d))

    rope_p, ropei_p = _rope_tables(np.arange(tp), HEAD_DIM), _rope_tables(np.arange(tp), D_IDX)
    rope_s, ropei_s = _rope_tables(np.full((rs,), past), HEAD_DIM), _rope_tables(np.full((rs,), past), D_IDX)

    xp = x_prompt
    xs = jnp.zeros((rs, d), F32).at[:db].set(x_sample[:, 0])
    tk_ffn = w_down.shape[1] // 4

    hp, hs = _norm_mod_call(xp, xs, g_mix[0], mod_p[0, :, :, 1], mod_p[0, :, :, 0], mod_s[0, :, 1], mod_s[0, :, 0], tm_full)

    outs = {n: [] for n in ("kip", "cp", "ks", "vs", "kis", "cs")}
    kp_all = jnp.zeros((depth, bp, tp, N_HEADS, HEAD_DIM), F32)
    vp_all = jnp.zeros((depth, bp, tp, N_HEADS, HEAD_DIM), F32)
    yp = ys = None
    for l in range(depth):
        last = l == depth - 1
        g_next = g_final if last else g_mix[l + 1]
        ln = l if last else l + 1
        sh1, sc1, gt1, sh2, sc2, gt2 = (mod_p[l, :, :, j] for j in range(6))
        sh1s, sc1s, gt1s, sh2s, sc2s, gt2s = (mod_s[l, :, j] for j in range(6))
        prev_s = jnp.zeros((rs, CONV_W - 1, d_conv), F32).at[:db].set(state_conv[l])

        a_in, conv_st, a_in_s, u_s = _conv_call(hp, hs, w_in_t, l, conv_w[l], jnp.zeros((bp, CONV_W - 1, d_conv), F32),
                                                prev_s[:, 0], prev_s[:, 1], tm, tn)
        q, q_s = _proj_call(hp, hs, w_in_t, l, cb_q, att_dim, rope_p, rope_s, tm, half=HEAD_DIM // 2,
                            outs=("bf16",), outs_s=("f32",))
        kp_all, k16, k_s = _proj_call(hp, hs, w_in_t, l, cb_k, att_dim, rope_p, rope_s, tm, half=HEAD_DIM // 2,
                                      outs=("heads", "bf16"), outs_s=("f32",), cache=kp_all)
        vp_all, v_t, v_s = _proj_call(hp, hs, w_in_t, l, cb_v, att_dim, rope_p, rope_s, tm, half=0,
                                      outs=("heads", "chunk_t"), outs_s=("f32",), tck=blk, cache=vp_all)
        qi_hm, qi_hm_s = _proj_call(hp, hs, w_in_t, l, cb_qi, idx_q, ropei_p, ropei_s, tm, half=D_IDX // 2,
                                    outs=("idx_heads",), outs_s=("idx_heads",))
        kw, ki16, wi_t, kw_s, ki16_s = _kidx_call(hp, hs, w_in_t, l, o_ki // LANES, ropei_p, ropei_s, tm)

        att = _dsa_call(q, qi_hm, wi_t, k16, v_t, ki16, topk_p, blk)
        qi_hb = qi_hm_s[:, :db].reshape(N_IDX_HEADS * db, D_IDX)
        w_hb = kw_s[:db, D_IDX:D_IDX + N_IDX_HEADS].T.reshape(N_IDX_HEADS * db, 1)
        scores = _pool_scores_call(qi_hb, w_hb, kidx_t, l, db, _pages_per_step(n_pool))
        rows = _select_call(page_table, scores.reshape(db, n_pool, PAGE_SIZE), qi_hb, w_hb, ki16_s[:db], topk_s)
        att_s = _gather_attend_call(rows.reshape(db, topk_s),
                                    q_s[:db].reshape(db, N_HEADS, HEAD_DIM),
                                    k_s[:db].reshape(db, N_HEADS, HEAD_DIM),
                                    v_s[:db].reshape(db, N_HEADS, HEAD_DIM),
                                    ck_rows, cv_rows, l, topk_s)
        att_s = jnp.zeros((rs, att_dim), BF16).at[:db].set(att_s.reshape(db, att_dim).astype(BF16))

        merged, merged_s = _merge_call(hp, a_in, att, hs, a_in_s, att_s, w_in_t, o_ga, w_pa, w_pb, l, tm_full, tn)
        xp, h2, xs, h2_s = _resid_norm_call(xp, merged, xs, merged_s, w_o16, l, g_ffn[l],
                                            gt1, sc2, sh2, gt1s, sc2s, sh2s, tm_full)
        act, act_s = _ffn_up_call(h2, h2_s, w_gate, w_up, l, tm, tn)
        xp, nxt, xs, nxt_s = _ffn_down_call(xp, act, xs, act_s, w_down16, l, g_next, gt2, mod_p[ln, :, :, 1],
                                            mod_p[ln, :, :, 0], gt2s, mod_s[ln, :, 1], mod_s[ln, :, 0],
                                            tm_full, tk_ffn, last)
        if last:
            yp, ys = nxt, nxt_s
        else:
            hp, hs = nxt, nxt_s
        outs["kip"].append(kw[:, :, :D_IDX].reshape(bp, tp // PAGE_SIZE, PAGE_SIZE, D_IDX))
        outs["cp"].append(conv_st)
        outs["ks"].append(k_s[:db].reshape(db, ts, N_HEADS, HEAD_DIM))
        outs["vs"].append(v_s[:db].reshape(db, ts, N_HEADS, HEAD_DIM))
        outs["kis"].append(kw_s[:db, :D_IDX].reshape(db, ts, D_IDX))
        outs["cs"].append(jnp.stack([prev_s[:db, 1], u_s[:db]], axis=1))

    page_shape = (depth, bp, tp // PAGE_SIZE, PAGE_SIZE, N_HEADS, HEAD_DIM)
    return (yp, ys[:db].reshape(db, ts, d),
            kp_all.reshape(page_shape), vp_all.reshape(page_shape), jnp.stack(outs["kip"]), jnp.stack(outs["cp"]),
            jnp.stack(outs["ks"]), jnp.stack(outs["vs"]), jnp.stack(outs["kis"]), jnp.stack(outs["cs"]))
```

```python
import functools

import jax
import jax.numpy as jnp
import numpy as np
from jax import lax
from jax.experimental import pallas as pl
from jax.experimental.pallas import tpu as pltpu

F32 = jnp.float32
BF16 = jnp.bfloat16
I32 = jnp.int32
I16 = jnp.int16

N_HEADS = 8
HEAD_DIM = 128
N_IDX_HEADS = 16
D_IDX = 64
CONV_W = 3
TOPK_MAX = 256
PAGE_SIZE = 128
ROPE_THETA = 10000.0
EPS = 1e-6
INDEX_SCALE = (D_IDX ** -0.5) * (N_IDX_HEADS ** -0.5)
ATT_SCALE = HEAD_DIM ** -0.5
EXP2_SCALE = ATT_SCALE * 1.4426950408889634

LANES = 128
SUBLANES = 8
VMEM_LIMIT = 56 * 1024 * 1024
INT_MIN = -2 ** 31
HALF16 = 2 ** 15
M_FLOOR = -1e30
ATT_BLOCK = 256
SAMPLE_ROWS = 16
ROW_TILE = 1024
FFN_UP_ROW_TILE = 2048
ROW_TILE_FULL = 512
COL_TILE = 512
MERGE_COL_TILE = 256


def _params(sem):
    return pltpu.CompilerParams(dimension_semantics=sem, vmem_limit_bytes=VMEM_LIMIT)


def _dot(a, b):
    return jnp.dot(a, b, preferred_element_type=F32)


def _dot_nt(a, b):
    return lax.dot_general(a, b, (((1,), (1,)), ((), ())), preferred_element_type=F32)


def _rmsnorm(x):
    return x * lax.rsqrt(jnp.mean(x * x, axis=-1, keepdims=True) + EPS)


def _sort_key(s):
    s = jnp.where(s == 0.0, 0.0, s)
    b = pltpu.bitcast(s, I32)
    return jnp.where(b < 0, b ^ 0x7FFFFFFF, b)


def _w_spec(k, tn, layer, col0=0, col_axis=None, row_axis=None):
    def index(*ids):
        return (layer, 0 if row_axis is None else ids[row_axis], col0 + (0 if col_axis is None else ids[col_axis]))
    return pl.BlockSpec((None, k, tn), index)


def _wt_spec(tn, k, layer, row0=0, row_axis=None):
    return pl.BlockSpec((None, tn, k), lambda *ids: (layer, row0 + (0 if row_axis is None else ids[row_axis]), 0))


def _wt_rows_spec(tn, k, layer, first_row, row_axis):
    assert first_row % SUBLANES == 0 and tn % SUBLANES == 0
    return pl.BlockSpec((pl.Element(1), pl.Element(tn), pl.Element(k)),
                        lambda *ids: (layer, pl.multiple_of(first_row + tn * ids[row_axis], SUBLANES), 0))


def _whole(a):
    return pl.BlockSpec(a.shape, lambda *ids: (0,) * a.ndim)


def _batch_vec(d, batch_axis):
    return pl.BlockSpec((None, 1, d), lambda *ids: (ids[batch_axis], 0, 0))


def _mod_kernel(c_ref, w_ref, b_ref, o_ref):
    c = c_ref[...]
    a = (c * jax.nn.sigmoid(c)).astype(BF16)
    o_ref[...] = _dot(a, w_ref[...].astype(BF16)) + b_ref[...]


def _mod_call(c_all, w_mod, b_mod):
    depth, d, n = w_mod.shape
    rows = c_all.shape[0]
    tn = 1024
    return pl.pallas_call(
        _mod_kernel,
        grid=(depth, n // tn),
        in_specs=[pl.BlockSpec((rows, d), lambda l, j: (0, 0)),
                  pl.BlockSpec((None, d, tn), lambda l, j: (l, 0, j)),
                  pl.BlockSpec((None, 1, tn), lambda l, j: (l, 0, j))],
        out_specs=pl.BlockSpec((None, rows, tn), lambda l, j: (l, 0, j)),
        out_shape=jax.ShapeDtypeStruct((depth, rows, n), F32),
        compiler_params=_params(("arbitrary", "arbitrary")),
        name="mod",
    )(c_all, w_mod, b_mod.reshape(depth, 1, n))


def _norm_mod(x, g, sc, sh):
    return (_rmsnorm(x) * g * (1.0 + sc) + sh).astype(BF16)


def _norm_mod_kernel(x_ref, xs_ref, g_ref, sc_ref, sh_ref, scs_ref, shs_ref, o_ref, os_ref):
    @pl.when((pl.program_id(0) == 0) & (pl.program_id(1) == 0))
    def _():
        os_ref[...] = _norm_mod(xs_ref[...], g_ref[...], scs_ref[...], shs_ref[...])

    o_ref[...] = _norm_mod(x_ref[...], g_ref[...], sc_ref[...], sh_ref[...])


def _norm_mod_call(x, xs, g, sc, sh, scs, shs, tm):
    gn, r, d = x.shape
    row = pl.BlockSpec((None, tm, d), lambda b, i: (b, i, 0))
    g = g.reshape(1, d)
    return pl.pallas_call(
        _norm_mod_kernel,
        grid=(gn, r // tm),
        in_specs=[row, _whole(xs), _whole(g), _batch_vec(d, 0), _batch_vec(d, 0), _whole(scs), _whole(shs)],
        out_specs=[row, _whole(xs)],
        out_shape=[jax.ShapeDtypeStruct((gn, r, d), BF16), jax.ShapeDtypeStruct(xs.shape, BF16)],
        compiler_params=_params(("arbitrary", "arbitrary")),
        name="norm_mod",
    )(x, xs, g, sc, sh, scs, shs)


def _conv_kernel(h_ref, hs_ref, wb_ref, wc_ref, wv_ref, cw_ref, prev_ref, p0_ref, p1_ref,
                 a_ref, st_ref, as_ref, us_ref, buf, w16, *, tm):
    b, i = pl.program_id(1), pl.program_id(2)
    cw = cw_ref[...]

    @pl.when((b == 0) & (i == 0))
    def _():
        for k, w_ref in enumerate((wb_ref, wc_ref, wv_ref)):
            w16[k] = w_ref[...].T.astype(BF16)
        hs = hs_ref[...]
        us = _dot(hs, w16[1]) * _dot(hs, w16[2])
        ys = cw[0:1] * p0_ref[...] + cw[1:2] * p1_ref[...] + cw[2:3] * us
        as_ref[...] = (_dot(hs, w16[0]) * ys).astype(BF16)
        us_ref[...] = us

    h = h_ref[...]
    bg = _dot(h, w16[0])
    u = _dot(h, w16[1]) * _dot(h, w16[2])

    @pl.when(i == 0)
    def _():
        buf[6:8, :] = prev_ref[...]

    buf[8:8 + tm, :] = u
    y = cw[0:1] * buf[6:6 + tm, :] + cw[1:2] * buf[7:7 + tm, :] + cw[2:3] * u
    a_ref[...] = (bg * y).astype(BF16)
    last2 = buf[6 + tm:8 + tm, :]
    buf[6:8, :] = last2
    st_ref[...] = last2


def _conv_call(h, hs, w_in, layer, cw, prev, p0, p1, tm, tn):
    gn, r, d = h.shape
    rs = hs.shape[0]
    n = cw.shape[1]
    nj = n // tn
    s_tile = pl.BlockSpec((rs, tn), lambda j, b, i: (0, j))
    return pl.pallas_call(
        functools.partial(_conv_kernel, tm=tm),
        grid=(nj, gn, r // tm),
        in_specs=[pl.BlockSpec((None, tm, d), lambda j, b, i: (b, i, 0)), _whole(hs),
                  _wt_spec(tn, d, layer, 0, 0), _wt_spec(tn, d, layer, nj, 0), _wt_spec(tn, d, layer, 2 * nj, 0),
                  pl.BlockSpec((CONV_W, tn), lambda j, b, i: (0, j)),
                  pl.BlockSpec((None, CONV_W - 1, tn), lambda j, b, i: (b, 0, j)),
                  s_tile, s_tile],
        out_specs=[pl.BlockSpec((None, tm, tn), lambda j, b, i: (b, i, j)),
                   pl.BlockSpec((None, CONV_W - 1, tn), lambda j, b, i: (b, 0, j)),
                   s_tile, s_tile],
        out_shape=[jax.ShapeDtypeStruct((gn, r, n), BF16), jax.ShapeDtypeStruct((gn, CONV_W - 1, n), F32),
                   jax.ShapeDtypeStruct((rs, n), BF16), jax.ShapeDtypeStruct((rs, n), F32)],
        scratch_shapes=[pltpu.VMEM((8 + tm, tn), F32), pltpu.VMEM((3, d, tn), BF16)],
        compiler_params=_params(("arbitrary", "arbitrary", "arbitrary")),
        name="conv",
    )(h, hs, w_in, w_in, w_in, cw, prev, p0, p1)


def _rope_group(x, cos, sin, half):
    if 2 * half == LANES:
        rot = pltpu.roll(x, half, 1)
    else:
        lane = lax.broadcasted_iota(I32, x.shape, 1)
        rot = jnp.where(lane % (2 * half) < half, pltpu.roll(x, LANES - half, 1), pltpu.roll(x, half, 1))
    return x * cos + rot * sin


def _proj_emit(acc, cos, sin, half, kinds, refs, tck):
    groups = []
    for c in range(acc.shape[1] // LANES):
        x = acc[:, c * LANES:(c + 1) * LANES]
        groups.append(_rope_group(x, cos, sin, half) if half else x)
    for kind, o in zip(kinds, refs):
        for c, x in enumerate(groups):
            if kind == "f32":
                o[:, c * LANES:(c + 1) * LANES] = x
            elif kind == "bf16":
                o[:, c * LANES:(c + 1) * LANES] = x.astype(BF16)
            elif kind == "heads":
                o[:, c, :] = x
            elif kind == "idx_heads":
                xb = x.astype(BF16)
                o[2 * c] = xb[:, :D_IDX]
                o[2 * c + 1] = xb[:, D_IDX:]
            elif kind == "chunk_t":
                for k in range(x.shape[0] // tck):
                    o[k, c * LANES:(c + 1) * LANES, :] = x[k * tck:(k + 1) * tck, :].T.astype(BF16)


def _proj_kernel(*refs, half, outs, outs_s, tck, aliased):
    h_ref, hs_ref, w_ref, cos_ref, sin_ref, cos_s_ref, sin_s_ref = refs[:7]
    rest = refs[7 + aliased:]
    out_refs, out_s_refs, w16 = rest[:len(outs)], rest[len(outs):len(outs) + len(outs_s)], rest[-1]

    @pl.when((pl.program_id(0) == 0) & (pl.program_id(1) == 0))
    def _():
        w16[...] = w_ref[...].T.astype(BF16)
        _proj_emit(_dot(hs_ref[...], w16[...]), cos_s_ref[...], sin_s_ref[...], half, outs_s, out_s_refs, None)

    _proj_emit(_dot(h_ref[...], w16[...]), cos_ref[...], sin_ref[...], half, outs, out_refs, tck)


def _proj_call(h, hs, w_in, layer, col0, n, rope, rope_s, tm, *, half, outs, outs_s, tck=None, cache=None):
    gn, r, d = h.shape
    rs = hs.shape[0]
    heads = n // LANES
    depth = None if cache is None else cache.shape[0]
    out_specs, out_shape = [], []
    for kind in outs:
        if kind == "bf16":
            out_specs.append(pl.BlockSpec((None, tm, n), lambda b, i: (b, i, 0)))
            out_shape.append(jax.ShapeDtypeStruct((gn, r, n), BF16))
        elif kind == "heads":
            out_specs.append(pl.BlockSpec((None, None, tm, heads, LANES), lambda b, i: (layer, b, i, 0, 0)))
            out_shape.append(jax.ShapeDtypeStruct((depth, gn, r, heads, LANES), F32))
        elif kind == "idx_heads":
            out_specs.append(pl.BlockSpec((None, n // D_IDX, tm, D_IDX), lambda b, i: (b, 0, i, 0)))
            out_shape.append(jax.ShapeDtypeStruct((gn, n // D_IDX, r, D_IDX), BF16))
        elif kind == "chunk_t":
            out_specs.append(pl.BlockSpec((None, tm // tck, n, tck), lambda b, i: (b, i, 0, 0)))
            out_shape.append(jax.ShapeDtypeStruct((gn, r // tck, n, tck), BF16))
    for kind in outs_s:
        shape, dtype = ((rs, n), F32) if kind == "f32" else ((n // D_IDX, rs, D_IDX), BF16)
        out_specs.append(pl.BlockSpec(shape, lambda b, i, nd=len(shape): (0,) * nd))
        out_shape.append(jax.ShapeDtypeStruct(shape, dtype))
    tab = pl.BlockSpec((tm, LANES), lambda b, i: (i, 0))
    in_specs = [pl.BlockSpec((None, tm, d), lambda b, i: (b, i, 0)), _whole(hs), _wt_spec(n, d, layer, col0),
                tab, tab, _whole(rope_s[0]), _whole(rope_s[1])]
    args = [h, hs, w_in, rope[0], rope[1], rope_s[0], rope_s[1]]
    aliases = {}
    if cache is not None:
        in_specs.append(pl.BlockSpec(memory_space=pl.ANY))
        args.append(cache)
        aliases = {7: outs.index("heads")}
    return pl.pallas_call(
        functools.partial(_proj_kernel, half=half, outs=outs, outs_s=outs_s, tck=tck, aliased=cache is not None),
        grid=(gn, r // tm),
        in_specs=in_specs, out_specs=out_specs, out_shape=out_shape,
        scratch_shapes=[pltpu.VMEM((d, n), BF16)],
        input_output_aliases=aliases,
        compiler_params=_params(("arbitrary", "arbitrary")),
        name="proj",
    )(*args)


def _kidx_rows(acc, cos, sin):
    roped = _rope_group(acc, cos, sin, D_IDX // 2)
    lane = lax.broadcasted_iota(I32, acc.shape, 1)
    return jnp.where(lane < D_IDX, roped, jnp.where(lane < D_IDX + N_IDX_HEADS, acc, 0.0))


def _kidx_kernel(h_ref, hs_ref, w_ref, cos_ref, sin_ref, cos_s_ref, sin_s_ref,
                 kw_ref, kb_ref, wt_ref, kws_ref, kbs_ref, w16):
    @pl.when((pl.program_id(0) == 0) & (pl.program_id(1) == 0))
    def _():
        w16[...] = w_ref[...].T.astype(BF16)
        kws = _kidx_rows(_dot(hs_ref[...], w16[...]), cos_s_ref[...], sin_s_ref[...])
        kws_ref[...] = kws
        kbs_ref[...] = kws[:, :D_IDX].astype(BF16)

    kw = _kidx_rows(_dot(h_ref[...], w16[...]), cos_ref[...], sin_ref[...])
    kw_ref[...] = kw
    kb_ref[...] = kw[:, :D_IDX].astype(BF16)
    wt_ref[...] = kw.T[D_IDX:D_IDX + N_IDX_HEADS, :]


def _kidx_call(h, hs, w_in, layer, col0, rope, rope_s, tm):
    gn, r, d = h.shape
    rs = hs.shape[0]
    tab = pl.BlockSpec((tm, LANES), lambda b, i: (i, 0))
    s128 = pl.BlockSpec((rs, LANES), lambda b, i: (0, 0))
    s64 = pl.BlockSpec((rs, D_IDX), lambda b, i: (0, 0))
    return pl.pallas_call(
        _kidx_kernel,
        grid=(gn, r // tm),
        in_specs=[pl.BlockSpec((None, tm, d), lambda b, i: (b, i, 0)), _whole(hs), _wt_spec(LANES, d, layer, col0),
                  tab, tab, s128, s128],
        out_specs=[pl.BlockSpec((None, tm, LANES), lambda b, i: (b, i, 0)),
                   pl.BlockSpec((None, tm, D_IDX), lambda b, i: (b, i, 0)),
                   pl.BlockSpec((None, N_IDX_HEADS, tm), lambda b, i: (b, 0, i)),
                   s128, s64],
        out_shape=[jax.ShapeDtypeStruct((gn, r, LANES), F32), jax.ShapeDtypeStruct((gn, r, D_IDX), BF16),
                   jax.ShapeDtypeStruct((gn, N_IDX_HEADS, r), F32),
                   jax.ShapeDtypeStruct((rs, LANES), F32), jax.ShapeDtypeStruct((rs, D_IDX), BF16)],
        scratch_shapes=[pltpu.VMEM((d, LANES), BF16)],
        compiler_params=_params(("arbitrary", "arbitrary")),
        name="kidx",
    )(h, hs, w_in, rope[0], rope[1], rope_s[0], rope_s[1])


def _strict_tri(n, dtype, lower):
    row = lax.broadcasted_iota(I32, (n, n), 0)
    col = lax.broadcasted_iota(I32, (n, n), 1)
    return jnp.where(col < row if lower else row < col, 1.0, 0.0).astype(dtype)


def _dsa_kernel(q_ref, qi_ref, wi_ref, k_ref, vt_ref, ki_ref, o_ref,
                key_scr, hi_scr, lo_scr, m_scr, l_scr, acc_scr, *, topk, ck, qb):
    i = pl.program_id(1)
    nch = ((i + 1) * qb + ck - 1) // ck
    qpos = i * qb + lax.broadcasted_iota(I32, (LANES, qb), 1)
    krow = lax.broadcasted_iota(I32, (LANES, qb), 0)
    w = wi_ref[...]

    def index_chunk(c, _):
        base = pl.multiple_of(c * ck, ck)
        for r in range(ck // LANES):
            rows = slice(r * LANES, (r + 1) * LANES)
            kc = ki_ref[pl.ds(base + r * LANES, LANES), :]
            acc = jnp.zeros((LANES, qb), F32)
            for h in range(N_IDX_HEADS):
                acc = acc + jnp.maximum(_dot_nt(kc, qi_ref[h]), 0.0) * w[h:h + 1, :]
            kpos = base + r * LANES + krow
            key = jnp.where(kpos <= qpos, _sort_key(acc * INDEX_SCALE), INT_MIN)
            key_scr[c, rows, :] = key
            hi_scr[c, rows, :] = lax.shift_right_arithmetic(key, 16).astype(I16)
            lo_scr[c, rows, :] = ((key & 0xFFFF) - HALF16).astype(I16)
        return 0

    lax.fori_loop(0, nch, index_chunk, 0)

    ge = lambda a, b: a >= b
    gt = lambda a, b: a > b
    pack = 2 * SUBLANES

    def count16(scr, pred, thr):
        thr = thr.astype(I16)

        def body(c, part):
            hit = jnp.where(pred(scr[c], thr), jnp.ones((), BF16), jnp.zeros((), BF16))
            terms = [hit[k * pack:(k + 1) * pack] for k in range(ck // pack)]
            while len(terms) > 1:
                terms = [a + b for a, b in zip(terms[0::2], terms[1::2])]
            return part + terms[0]

        part = lax.fori_loop(0, nch, body, jnp.zeros((pack, qb), BF16))
        return jnp.sum(part.astype(F32), axis=0, keepdims=True).astype(I32)

    def search16(scr, need):
        def step(it, t):
            cand = t + jnp.left_shift(jnp.int32(1), 15 - it)
            return jnp.where(count16(scr, ge, cand) >= need, cand, t)
        return lax.fori_loop(0, 16, step, jnp.full((1, qb), -HALF16, I32))

    t_hi = search16(hi_scr, topk)
    need_lo = topk - count16(hi_scr, gt, t_hi)
    t_hi16 = t_hi.astype(I16)

    def keep_band(c, _):
        lo_scr[c] = jnp.where(hi_scr[c] == t_hi16, lo_scr[c], jnp.full((), -HALF16, I16))
        return 0

    lax.fori_loop(0, nch, keep_band, 0)
    t_lo = search16(lo_scr, need_lo)
    t = t_hi * (2 * HALF16) + (t_lo + HALF16)
    t = jnp.maximum(t, INT_MIN + 1)

    def count(pred, thr):
        def body(c, cnt):
            hit = jnp.where(pred(key_scr[c], thr), 1, 0)
            return cnt + jnp.sum(hit.reshape(ck // SUBLANES, SUBLANES, qb), axis=0)
        cnt = lax.fori_loop(0, nch, body, jnp.zeros((SUBLANES, qb), I32))
        return jnp.sum(cnt, axis=0, keepdims=True)

    c_ge = count(ge, t)

    @pl.when(jnp.max(c_ge) > topk)
    def _():
        need = (topk - count(gt, t)).astype(F32)
        lower = _strict_tri(ck, BF16, lower=True)

        def body(c, carry):
            kc = key_scr[c]
            eq = kc == t
            eqf = jnp.where(eq, 1.0, 0.0)
            before = _dot(lower, eqf.astype(BF16)) + carry
            key_scr[c] = jnp.where(eq & (before >= need), INT_MIN, kc)
            return carry + jnp.sum(eqf, axis=0, keepdims=True)

        lax.fori_loop(0, nch, body, jnp.zeros((1, qb), F32))

    m_scr[...] = jnp.full(m_scr.shape, M_FLOOR, F32)
    l_scr[...] = jnp.zeros(l_scr.shape, F32)
    acc_scr[...] = jnp.zeros(acc_scr.shape, F32)

    def attend_chunk(c, _):
        base = pl.multiple_of(c * ck, ck)
        sel = key_scr[c] >= t
        heads = [slice(h * HEAD_DIM, (h + 1) * HEAD_DIM) for h in range(N_HEADS)]
        raw = [_dot_nt(k_ref[pl.ds(base, ck), cols], q_ref[:, cols]) for cols in heads]
        probs = []
        for h in range(N_HEADS):
            s = jnp.where(sel, raw[h], -jnp.inf)
            m_old = m_scr[h]
            m_new = jnp.maximum(m_old, jnp.max(s, axis=0, keepdims=True))
            alpha = jnp.exp2((m_old - m_new) * EXP2_SCALE)
            p = jnp.exp2((s - m_new) * EXP2_SCALE)
            l_scr[h] = alpha * l_scr[h] + jnp.sum(p, axis=0, keepdims=True)
            m_scr[h] = m_new
            probs.append((alpha, p.astype(BF16)))
        for h in range(N_HEADS):
            alpha, p = probs[h]
            acc_scr[h] = alpha * acc_scr[h] + _dot(vt_ref[c, heads[h], :], p)
        return 0

    lax.fori_loop(0, nch, attend_chunk, 0)
    for h in range(N_HEADS):
        o_ref[:, h * HEAD_DIM:(h + 1) * HEAD_DIM] = (acc_scr[h] / l_scr[h]).T.astype(o_ref.dtype)


def _dsa_call(q, qi_hm, wi_t, k, v_t, ki, topk, blk):
    bn, t, a = q.shape
    assert t <= 256 * 2 * SUBLANES, "the threshold search counts in bf16: at most 256 keys per packed-row entry"
    full = lambda shape: pl.BlockSpec((None,) + shape, lambda b, i: (b,) + (0,) * len(shape))
    return pl.pallas_call(
        functools.partial(_dsa_kernel, topk=topk, ck=blk, qb=blk),
        grid=(bn, t // blk),
        in_specs=[pl.BlockSpec((None, blk, a), lambda b, i: (b, i, 0)),
                  pl.BlockSpec((None, N_IDX_HEADS, blk, D_IDX), lambda b, i: (b, 0, i, 0)),
                  pl.BlockSpec((None, N_IDX_HEADS, blk), lambda b, i: (b, 0, i)),
                  full((t, a)), full((t // blk, a, blk)), full((t, D_IDX))],
        out_specs=pl.BlockSpec((None, blk, a), lambda b, i: (b, i, 0)),
        out_shape=jax.ShapeDtypeStruct((bn, t, a), BF16),
        scratch_shapes=[pltpu.VMEM((t // blk, blk, blk), I32),
                        pltpu.VMEM((t // blk, blk, blk), I16),
                        pltpu.VMEM((t // blk, blk, blk), I16),
                        pltpu.VMEM((N_HEADS, 1, blk), F32),
                        pltpu.VMEM((N_HEADS, 1, blk), F32),
                        pltpu.VMEM((N_HEADS, HEAD_DIM, blk), F32)],
        compiler_params=_params(("arbitrary", "arbitrary")),
        name="dsa_prompt",
    )(q, qi_hm, wi_t, k, v_t, ki)


def _pool_scores_kernel(qi_ref, w_ref, ck_ref, o_ref, *, nb, group):
    pages, _, page = ck_ref.shape
    qi, w = qi_ref[...], w_ref[...]
    for p in range(0, pages, group):
        kc = jnp.concatenate([ck_ref[p + g] for g in range(group)], axis=1).astype(BF16)
        d = jnp.maximum(_dot(qi, kc), 0.0) * w
        s = d[0:nb]
        for h in range(1, N_IDX_HEADS):
            s = s + d[h * nb:(h + 1) * nb]
        o_ref[:, p * page:(p + group) * page] = s * INDEX_SCALE


def _pool_scores_call(qi_hb, w_hb, kidx_t, layer, nb, pages_per_step):
    depth, n_pool, di, page = kidx_t.shape
    group = 2 if pages_per_step % 2 == 0 else 1
    return pl.pallas_call(
        functools.partial(_pool_scores_kernel, nb=nb, group=group),
        grid=(n_pool // pages_per_step,),
        in_specs=[pl.BlockSpec(qi_hb.shape, lambda j: (0, 0)),
                  pl.BlockSpec(w_hb.shape, lambda j: (0, 0)),
                  pl.BlockSpec((None, pages_per_step, di, page), lambda j: (layer, j, 0, 0))],
        out_specs=pl.BlockSpec((nb, pages_per_step * page), lambda j: (0, j)),
        out_shape=jax.ShapeDtypeStruct((nb, n_pool * page), F32),
        compiler_params=_params(("arbitrary",)),
        name="pool_scores",
    )(qi_hb, w_hb, kidx_t)


def _select_kernel(pt_ref, sc_ref, ptv_ref, qi_ref, w_ref, kn_ref, row_ref, s_scr, *, topk, n_pages, nb):
    def gather_page(p, _):
        for b in range(nb):
            s_scr[b, pl.ds(p, 1), :] = sc_ref[b, pl.ds(pt_ref[b, p], 1), :]
        return 0

    lax.fori_loop(0, n_pages, gather_page, 0)
    key = _sort_key(s_scr[...])

    def total(x):
        return jnp.sum(jnp.sum(x, axis=1, keepdims=True), axis=2, keepdims=True)

    d = jnp.maximum(_dot_nt(qi_ref[...], kn_ref[...]), 0.0) * w_ref[...]
    row = lax.broadcasted_iota(I32, d.shape, 0)
    col = lax.broadcasted_iota(I32, d.shape, 1)
    s_new = jnp.stack([jnp.sum(jnp.where((row % nb == b) & (col == b), d, 0.0), keepdims=True) for b in range(nb)])
    key_new = _sort_key(s_new * INDEX_SCALE)

    def search(it, t):
        cand = t + jnp.left_shift(jnp.int32(1), 31 - it)
        n_ge = total(jnp.where(key >= cand, 1, 0)) + jnp.where(key_new >= cand, 1, 0)
        return jnp.where(n_ge >= topk, cand, t)

    t = lax.fori_loop(0, 32, search, jnp.full((nb, 1, 1), INT_MIN, I32))
    t = jnp.maximum(t, INT_MIN + 1)

    upper = _strict_tri(PAGE_SIZE, BF16, lower=False)
    lower = _strict_tri(n_pages, BF16, lower=True)
    ones = jnp.ones((SUBLANES, PAGE_SIZE), BF16)
    incl = jnp.where(lax.broadcasted_iota(I32, (n_pages, n_pages), 0) <= lax.broadcasted_iota(I32, (n_pages, n_pages), 1),
                     1.0, 0.0).astype(BF16)
    slot = lax.broadcasted_iota(I32, (topk, PAGE_SIZE), 0).astype(F32)
    lane = lax.broadcasted_iota(I32, (topk, PAGE_SIZE), 1)
    page_lane = lax.broadcasted_iota(I32, (topk, n_pages), 1)

    def ranks(f16):
        per_page = jnp.broadcast_to(jnp.sum(f16.astype(F32), axis=1, keepdims=True), f16.shape)
        return _dot(f16, upper), _dot(lower, per_page.astype(BF16))

    for b in range(nb):
        kb, tb, knb = key[b], t[b], key_new[b]
        n_gt = jnp.sum(jnp.where(kb > tb, 1.0, 0.0), keepdims=True) + jnp.where(knb > tb, 1.0, 0.0)
        need = topk - n_gt
        eq = kb == tb
        eq16 = jnp.where(eq, 1.0, 0.0).astype(BF16)
        eq_in, eq_before = ranks(eq16)
        sel = (kb > tb) | (eq & (eq_in + eq_before < need))
        sel_new = (knb > tb) | ((knb == tb) & (jnp.sum(eq16.astype(F32), keepdims=True) < need))

        sel16 = jnp.where(sel, 1.0, 0.0).astype(BF16)
        in_page, before_page = ranks(sel16)
        n_sel = jnp.sum(sel16.astype(F32), keepdims=True)
        cum = _dot(_dot_nt(ones, sel16).astype(BF16), incl)[0:1, :]
        page_r = jnp.sum(jnp.where(cum <= slot[:, 0:1], 1.0, 0.0), axis=1, keepdims=True).astype(I32)
        hit = page_lane == page_r
        onehot = jnp.where(hit, 1.0, 0.0).astype(BF16)
        got = _dot(onehot, jnp.where(sel, in_page, -1.0).astype(BF16))
        within = slot - _dot(onehot, before_page.astype(BF16))
        off = jnp.sum(jnp.where(got == within, lane, 0), axis=1, keepdims=True)
        phys = jnp.sum(jnp.where(hit, ptv_ref[b:b + 1, :], 0), axis=1, keepdims=True)
        rows = jnp.where(slot[:, 0:1] < n_sel, phys * PAGE_SIZE + off, 0)
        row_ref[b] = rows - jnp.where(sel_new & (slot[:, 0:1] == n_sel), 1, 0)


def _select_call(page_table, scores, qi_hb, w_hb, k_new, topk):
    nb, n_pages = page_table.shape
    whole = lambda a: pl.BlockSpec(a.shape, lambda i, pt: (0,) * a.ndim)
    return pl.pallas_call(
        functools.partial(_select_kernel, topk=topk, n_pages=n_pages, nb=nb),
        grid_spec=pltpu.PrefetchScalarGridSpec(
            num_scalar_prefetch=1, grid=(1,),
            in_specs=[whole(scores), whole(page_table), whole(qi_hb), whole(w_hb), whole(k_new)],
            out_specs=pl.BlockSpec((nb, topk, 1), lambda i, pt: (0, 0, 0)),
            scratch_shapes=[pltpu.VMEM((nb, n_pages, PAGE_SIZE), F32)]),
        out_shape=jax.ShapeDtypeStruct((nb, topk, 1), I32),
        compiler_params=_params(("arbitrary",)),
        name="select",
    )(page_table, scores, page_table, qi_hb, w_hb, k_new)


def _gather_attend_kernel(row_ref, q_ref, kn_ref, vn_ref, ck_hbm, cv_hbm, o_ref, kbuf, vbuf, sem, *, layer, topk):
    b = pl.program_id(0)

    def copies(r):
        src = jnp.maximum(row_ref[b, r], 0)
        return (pltpu.make_async_copy(ck_hbm.at[layer, src], kbuf.at[r], sem.at[0]),
                pltpu.make_async_copy(cv_hbm.at[layer, src], vbuf.at[r], sem.at[1]))

    def start(r, _):
        for cp in copies(r):
            cp.start()
        return 0

    def wait(r, _):
        for cp in copies(r):
            cp.wait()
        return 0

    lax.fori_loop(0, topk, start, 0)
    lax.fori_loop(0, topk, wait, 0)

    @pl.when(row_ref[b, topk - 1] < 0)
    def _():
        kbuf[topk - 1] = kn_ref[...]
        vbuf[topk - 1] = vn_ref[...]

    q = q_ref[...]
    s = jnp.sum(kbuf[...] * q[None], axis=-1, keepdims=True) * ATT_SCALE
    p = jnp.exp(s - jnp.max(s, axis=0, keepdims=True))
    p = p / jnp.sum(p, axis=0, keepdims=True)
    o_ref[...] = jnp.sum(p * vbuf[...], axis=0)


def _gather_attend_call(rows, q, k_new, v_new, cache_k, cache_v, layer, topk):
    nb = rows.shape[0]
    any_spec = pl.BlockSpec(memory_space=pl.ANY)
    per_sample = pl.BlockSpec((None, N_HEADS, HEAD_DIM), lambda b, rows: (b, 0, 0))
    return pl.pallas_call(
        functools.partial(_gather_attend_kernel, layer=layer, topk=topk),
        grid_spec=pltpu.PrefetchScalarGridSpec(
            num_scalar_prefetch=1, grid=(nb,),
            in_specs=[per_sample, per_sample, per_sample, any_spec, any_spec],
            out_specs=per_sample,
            scratch_shapes=[pltpu.VMEM((topk, N_HEADS, HEAD_DIM), F32),
                            pltpu.VMEM((topk, N_HEADS, HEAD_DIM), F32),
                            pltpu.SemaphoreType.DMA((2,))]),
        out_shape=jax.ShapeDtypeStruct((nb, N_HEADS, HEAD_DIM), F32),
        compiler_params=_params(("arbitrary",)),
        name="gather_attend",
    )(rows, q, k_new, v_new, cache_k, cache_v)


def _merge_kernel(h_ref, a_ref, t_ref, hs_ref, as_ref, ts_ref, wga_ref, wgb_ref, wpa_ref, wpb_ref,
                  o_ref, os_ref, wga16, wgb16, wpa16, wpb16):
    def mix(h, a, t):
        ya = _dot(a, wpa16[...])
        yb = _dot(t, wpb16[...])
        ga = jax.nn.sigmoid(_dot(h, wga16[...]))
        gb = jax.nn.sigmoid(_dot(h, wgb16[...]))
        return (ga * ya + gb * yb).astype(BF16)

    @pl.when((pl.program_id(1) == 0) & (pl.program_id(2) == 0))
    def _():
        wga16[...] = wga_ref[0].T.astype(BF16)
        wgb16[...] = wgb_ref[0].T.astype(BF16)
        wpa16[...] = wpa_ref[...].astype(BF16)
        wpb16[...] = wpb_ref[...].astype(BF16)
        os_ref[...] = mix(hs_ref[...], as_ref[...], ts_ref[...])

    o_ref[...] = mix(h_ref[...], a_ref[...], t_ref[...])


def _merge_call(h, a_in, att, hs, as_in, att_s, w_in, gate_row, w_pa, w_pb, layer, tm, tn):
    gn, r, d = h.shape
    a = a_in.shape[2]
    rs = hs.shape[0]
    nj = d // tn
    rows = lambda w: pl.BlockSpec((None, tm, w), lambda j, b, i: (b, i, 0))
    s_out = pl.BlockSpec((rs, tn), lambda j, b, i: (0, j))
    return pl.pallas_call(
        _merge_kernel,
        grid=(nj, gn, r // tm),
        in_specs=[rows(d), rows(a), rows(a), _whole(hs), _whole(as_in), _whole(att_s),
                  _wt_rows_spec(tn, d, layer, gate_row, 0), _wt_rows_spec(tn, d, layer, gate_row + d, 0),
                  _w_spec(a, tn, layer, 0, 0), _w_spec(a, tn, layer, 0, 0)],
        out_specs=[pl.BlockSpec((None, tm, tn), lambda j, b, i: (b, i, j)), s_out],
        out_shape=[jax.ShapeDtypeStruct((gn, r, d), BF16), jax.ShapeDtypeStruct((rs, d), BF16)],
        scratch_shapes=[pltpu.VMEM((d, tn), BF16), pltpu.VMEM((d, tn), BF16),
                        pltpu.VMEM((a, tn), BF16), pltpu.VMEM((a, tn), BF16)],
        compiler_params=_params(("arbitrary", "arbitrary", "arbitrary")),
        name="merge",
    )(h, a_in, att, hs, as_in, att_s, w_in, w_in, w_pa, w_pb)


def _resid_norm_kernel(x_ref, m_ref, xs_ref, ms_ref, w_ref, g_ref, gt_ref, sc_ref, sh_ref, gts_ref, scs_ref, shs_ref,
                       x_out, h_out, xs_out, hs_out):
    def step(x, m, gt, sc, sh):
        x = x + gt * _dot(m, w_ref[...])
        return x, _norm_mod(x, g_ref[...], sc, sh)

    @pl.when((pl.program_id(0) == 0) & (pl.program_id(1) == 0))
    def _():
        xs_out[...], hs_out[...] = step(xs_ref[...], ms_ref[...], gts_ref[...], scs_ref[...], shs_ref[...])

    x_out[...], h_out[...] = step(x_ref[...], m_ref[...], gt_ref[...], sc_ref[...], sh_ref[...])


def _resid_norm_call(x, m, xs, ms, w_o, layer, g, gt, sc, sh, gts, scs, shs, tm):
    gn, r, d = x.shape
    row = pl.BlockSpec((None, tm, d), lambda b, i: (b, i, 0))
    g = g.reshape(1, d)
    return pl.pallas_call(
        _resid_norm_kernel,
        grid=(gn, r // tm),
        in_specs=[row, row, _whole(xs), _whole(ms), _w_spec(d, d, layer), _whole(g),
                  _batch_vec(d, 0), _batch_vec(d, 0), _batch_vec(d, 0), _whole(gts), _whole(scs), _whole(shs)],
        out_specs=[row, row, _whole(xs), _whole(xs)],
        out_shape=[jax.ShapeDtypeStruct((gn, r, d), F32), jax.ShapeDtypeStruct((gn, r, d), BF16),
                   jax.ShapeDtypeStruct(xs.shape, F32), jax.ShapeDtypeStruct(xs.shape, BF16)],
        compiler_params=_params(("arbitrary", "arbitrary")),
        name="resid_norm",
    )(x, m, xs, ms, w_o, g, gt, sc, sh, gts, scs, shs)


def _ffn_up_kernel(h_ref, hs_ref, wg_ref, wu_ref, o_ref, os_ref, wg16, wu16):
    def act(h):
        g = _dot(h, wg16[...])
        return (g * jax.nn.sigmoid(g) * _dot(h, wu16[...])).astype(BF16)

    @pl.when((pl.program_id(1) == 0) & (pl.program_id(2) == 0))
    def _():
        wg16[...] = wg_ref[...].astype(BF16)
        wu16[...] = wu_ref[...].astype(BF16)
        os_ref[...] = act(hs_ref[...])

    o_ref[...] = act(h_ref[...])


def _ffn_up_call(h, hs, w_gate, w_up, layer, tm, tn):
    gn, r, d = h.shape
    rs = hs.shape[0]
    f = w_gate.shape[2]
    return pl.pallas_call(
        _ffn_up_kernel,
        grid=(f // tn, gn, r // tm),
        in_specs=[pl.BlockSpec((None, tm, d), lambda j, b, i: (b, i, 0)), _whole(hs),
                  _w_spec(d, tn, layer, 0, 0), _w_spec(d, tn, layer, 0, 0)],
        out_specs=[pl.BlockSpec((None, tm, tn), lambda j, b, i: (b, i, j)),
                   pl.BlockSpec((rs, tn), lambda j, b, i: (0, j))],
        out_shape=[jax.ShapeDtypeStruct((gn, r, f), BF16), jax.ShapeDtypeStruct((rs, f), BF16)],
        scratch_shapes=[pltpu.VMEM((d, tn), BF16), pltpu.VMEM((d, tn), BF16)],
        compiler_params=_params(("arbitrary", "arbitrary", "arbitrary")),
        name="ffn_up",
    )(h, hs, w_gate, w_up)


def _ffn_down_kernel(x_ref, a_ref, xs_ref, as_ref, w_ref, g_ref, gt_ref, sc_ref, sh_ref, gts_ref, scs_ref, shs_ref,
                     x_out, n_out, xs_out, ns_out, acc, acc_s, *, final):
    k = pl.program_id(2)
    last = k == pl.num_programs(2) - 1
    with_samples = (pl.program_id(0) == 0) & (pl.program_id(1) == 0)

    def finish(x, gt, sc, sh, total, x_o, n_o):
        x = x + gt * total
        x_o[...] = x
        n_o[...] = (_rmsnorm(x) * g_ref[...]).astype(n_o.dtype) if final else _norm_mod(x, g_ref[...], sc, sh)

    @pl.when(k == 0)
    def _():
        acc[...] = jnp.zeros(acc.shape, F32)

    acc[...] += _dot(a_ref[...], w_ref[...])

    @pl.when(with_samples & (k == 0))
    def _():
        acc_s[...] = jnp.zeros(acc_s.shape, F32)

    @pl.when(with_samples)
    def _():
        acc_s[...] += _dot(as_ref[...], w_ref[...])

    @pl.when(with_samples & last)
    def _():
        finish(xs_ref[...], gts_ref[...], scs_ref[...], shs_ref[...], acc_s[...], xs_out, ns_out)

    @pl.when(last)
    def _():
        finish(x_ref[...], gt_ref[...], sc_ref[...], sh_ref[...], acc[...], x_out, n_out)


def _ffn_down_call(x, act, xs, act_s, w_down, layer, g, gt, sc, sh, gts, scs, shs, tm, tk, final):
    gn, r, d = x.shape
    rs = xs.shape[0]
    f = act.shape[2]
    row = pl.BlockSpec((None, tm, d), lambda b, i, k: (b, i, 0))
    g = g.reshape(1, d)
    n_dtype = F32 if final else BF16
    return pl.pallas_call(
        functools.partial(_ffn_down_kernel, final=final),
        grid=(gn, r // tm, f // tk),
        in_specs=[row, pl.BlockSpec((None, tm, tk), lambda b, i, k: (b, i, k)),
                  _whole(xs), pl.BlockSpec((rs, tk), lambda b, i, k: (0, k)),
                  _w_spec(tk, d, layer, row_axis=2), _whole(g),
                  _batch_vec(d, 0), _batch_vec(d, 0), _batch_vec(d, 0), _whole(gts), _whole(scs), _whole(shs)],
        out_specs=[row, row, _whole(xs), _whole(xs)],
        out_shape=[jax.ShapeDtypeStruct((gn, r, d), F32), jax.ShapeDtypeStruct((gn, r, d), n_dtype),
                   jax.ShapeDtypeStruct(xs.shape, F32), jax.ShapeDtypeStruct(xs.shape, n_dtype)],
        scratch_shapes=[pltpu.VMEM((tm, d), F32), pltpu.VMEM((rs, d), F32)],
        compiler_params=_params(("arbitrary", "arbitrary", "arbitrary")),
        name="ffn_down",
    )(x, act, xs, act_s, w_down, g, gt, sc, sh, gts, scs, shs)


def _rope_tables(pos, d):
    inv_freq = ROPE_THETA ** (-np.arange(0, d, 2, dtype=np.float64) / d)
    ang = np.asarray(pos, np.float64)[:, None] * inv_freq[None, :]
    cos, sin = np.cos(ang), np.sin(ang)
    reps = LANES // d
    return (jnp.asarray(np.tile(np.concatenate([cos, cos], axis=1), (1, reps)), F32),
            jnp.asarray(np.tile(np.concatenate([-sin, sin], axis=1), (1, reps)), F32))


def _pages_per_step(n_pool):
    return max(p for p in range(1, 33) if n_pool % p == 0)


def kernel(x_prompt, x_sample, c_prompt, c_sample, cache_k, cache_v, cache_kidx, state_conv, page_table,
           w_mod, b_mod, g_mix, g_ffn, w_in, conv_w, w_pa, w_pb, w_o, w_gate, w_up, w_down, g_final):
    bp, tp, d = x_prompt.shape
    db, ts, _ = x_sample.shape
    assert ts == 1, "the sample path handles one new token per sequence"
    depth = w_mod.shape[0]
    n_pages = page_table.shape[1]
    past = n_pages * PAGE_SIZE
    att_dim = N_HEADS * HEAD_DIM
    d_conv = d // 2
    idx_q = N_IDX_HEADS * D_IDX
    topk_p = min(TOPK_MAX, tp // 4)
    topk_s = min(TOPK_MAX, (past + ts) // 4)
    rs = SAMPLE_ROWS
    blk = ATT_BLOCK if tp % ATT_BLOCK == 0 else LANES
    tm, tm_full, tn = min(tp, ROW_TILE), min(tp, ROW_TILE_FULL), COL_TILE
    tm_up = min(tp, FFN_UP_ROW_TILE)
    assert db <= rs and d_conv == att_dim == idx_q
    assert all(tp % rows == 0 for rows in (blk, tm, tm_full, tm_up))

    cb_q, cb_k, cb_v, cb_qi = 3, 4, 5, 6
    o_ki = 7 * att_dim
    o_ga = o_ki + D_IDX + N_IDX_HEADS
    assert o_ki % LANES == 0

    n_pool = cache_kidx.shape[1]
    kidx_t = jnp.swapaxes(cache_kidx, 2, 3)
    ck_rows = cache_k.reshape(depth, n_pool * PAGE_SIZE, N_HEADS, HEAD_DIM)
    cv_rows = cache_v.reshape(depth, n_pool * PAGE_SIZE, N_HEADS, HEAD_DIM)

    w_in_t = jnp.swapaxes(w_in, 1, 2)
    w_o16, w_down16 = w_o.astype(BF16), w_down.astype(BF16)

    c_all = jnp.zeros((rs, d), F32).at[:bp].set(c_prompt).at[bp:bp + db].set(c_sample)
    mod = _mod_call(c_all, w_mod, b_mod)
    mod_p = mod[:, :bp].reshape(depth, bp, 1, 6, d)
    mod_s = jnp.zeros((depth, rs, 6, d), F32).at[:, :db].set(mod[:, bp:bp + db].reshape(depth, db, 6, d))

    rope_p, ropei_p = _rope_tables(np.arange(tp), HEAD_DIM), _rope_tables(np.arange(tp), D_IDX)
    rope_s, ropei_s = _rope_tables(np.full((rs,), past), HEAD_DIM), _rope_tables(np.full((rs,), past), D_IDX)

    xp = x_prompt
    xs = jnp.zeros((rs, d), F32).at[:db].set(x_sample[:, 0])
    tk_ffn = w_down.shape[1] // 4

    hp, hs = _norm_mod_call(xp, xs, g_mix[0], mod_p[0, :, :, 1], mod_p[0, :, :, 0], mod_s[0, :, 1], mod_s[0, :, 0], tm_full)

    outs = {n: [] for n in ("kip", "cp", "ks", "vs", "kis", "cs")}
    kp_all = jnp.zeros((depth, bp, tp, N_HEADS, HEAD_DIM), F32)
    vp_all = jnp.zeros((depth, bp, tp, N_HEADS, HEAD_DIM), F32)
    yp = ys = None
    for l in range(depth):
        last = l == depth - 1
        g_next = g_final if last else g_mix[l + 1]
        ln = l if last else l + 1
        sh1, sc1, gt1, sh2, sc2, gt2 = (mod_p[l, :, :, j] for j in range(6))
        sh1s, sc1s, gt1s, sh2s, sc2s, gt2s = (mod_s[l, :, j] for j in range(6))
        prev_s = jnp.zeros((rs, CONV_W - 1, d_conv), F32).at[:db].set(state_conv[l])

        a_in, conv_st, a_in_s, u_s = _conv_call(hp, hs, w_in_t, l, conv_w[l], jnp.zeros((bp, CONV_W - 1, d_conv), F32),
                                                prev_s[:, 0], prev_s[:, 1], tm, tn)
        q, q_s = _proj_call(hp, hs, w_in_t, l, cb_q, att_dim, rope_p, rope_s, tm, half=HEAD_DIM // 2,
                            outs=("bf16",), outs_s=("f32",))
        kp_all, k16, k_s = _proj_call(hp, hs, w_in_t, l, cb_k, att_dim, rope_p, rope_s, tm, half=HEAD_DIM // 2,
                                      outs=("heads", "bf16"), outs_s=("f32",), cache=kp_all)
        vp_all, v_t, v_s = _proj_call(hp, hs, w_in_t, l, cb_v, att_dim, rope_p, rope_s, tm, half=0,
                                      outs=("heads", "chunk_t"), outs_s=("f32",), tck=blk, cache=vp_all)
        qi_hm, qi_hm_s = _proj_call(hp, hs, w_in_t, l, cb_qi, idx_q, ropei_p, ropei_s, tm, half=D_IDX // 2,
                                    outs=("idx_heads",), outs_s=("idx_heads",))
        kw, ki16, wi_t, kw_s, ki16_s = _kidx_call(hp, hs, w_in_t, l, o_ki // LANES, ropei_p, ropei_s, tm)

        att = _dsa_call(q, qi_hm, wi_t, k16, v_t, ki16, topk_p, blk)
        qi_hb = qi_hm_s[:, :db].reshape(N_IDX_HEADS * db, D_IDX)
        w_hb = kw_s[:db, D_IDX:D_IDX + N_IDX_HEADS].T.reshape(N_IDX_HEADS * db, 1)
        scores = _pool_scores_call(qi_hb, w_hb, kidx_t, l, db, _pages_per_step(n_pool))
        rows = _select_call(page_table, scores.reshape(db, n_pool, PAGE_SIZE), qi_hb, w_hb, ki16_s[:db], topk_s)
        att_s = _gather_attend_call(rows.reshape(db, topk_s),
                                    q_s[:db].reshape(db, N_HEADS, HEAD_DIM),
                                    k_s[:db].reshape(db, N_HEADS, HEAD_DIM),
                                    v_s[:db].reshape(db, N_HEADS, HEAD_DIM),
                                    ck_rows, cv_rows, l, topk_s)
        att_s = jnp.zeros((rs, att_dim), BF16).at[:db].set(att_s.reshape(db, att_dim).astype(BF16))

        merged, merged_s = _merge_call(hp, a_in, att, hs, a_in_s, att_s, w_in_t, o_ga, w_pa, w_pb, l, tm,
                                       MERGE_COL_TILE)
        xp, h2, xs, h2_s = _resid_norm_call(xp, merged, xs, merged_s, w_o16, l, g_ffn[l],
                                            gt1, sc2, sh2, gt1s, sc2s, sh2s, tm_full)
        act, act_s = _ffn_up_call(h2, h2_s, w_gate, w_up, l, tm_up, tn)
        xp, nxt, xs, nxt_s = _ffn_down_call(xp, act, xs, act_s, w_down16, l, g_next, gt2, mod_p[ln, :, :, 1],
                                            mod_p[ln, :, :, 0], gt2s, mod_s[ln, :, 1], mod_s[ln, :, 0],
                                            tm_full, tk_ffn, last)
        if last:
            yp, ys = nxt, nxt_s
        else:
            hp, hs = nxt, nxt_s
        outs["kip"].append(kw[:, :, :D_IDX].reshape(bp, tp // PAGE_SIZE, PAGE_SIZE, D_IDX))
        outs["cp"].append(conv_st)
        outs["ks"].append(k_s[:db].reshape(db, ts, N_HEADS, HEAD_DIM))
        outs["vs"].append(v_s[:db].reshape(db, ts, N_HEADS, HEAD_DIM))
        outs["kis"].append(kw_s[:db, :D_IDX].reshape(db, ts, D_IDX))
        outs["cs"].append(jnp.stack([prev_s[:db, 1], u_s[:db]], axis=1))

    page_shape = (depth, bp, tp // PAGE_SIZE, PAGE_SIZE, N_HEADS, HEAD_DIM)
    return (yp, ys[:db].reshape(db, ts, d),
            kp_all.reshape(page_shape), vp_all.reshape(page_shape), jnp.stack(outs["kip"]), jnp.stack(outs["cp"]),
            jnp.stack(outs["ks"]), jnp.stack(outs["vs"]), jnp.stack(outs["kis"]), jnp.stack(outs["cs"]))
```

```python
import functools

import jax
import jax.numpy as jnp
import numpy as np
from jax import lax
from jax.experimental import pallas as pl
from jax.experimental.pallas import tpu as pltpu

F32 = jnp.float32
BF16 = jnp.bfloat16
I32 = jnp.int32
I16 = jnp.int16

N_HEADS = 8
HEAD_DIM = 128
N_IDX_HEADS = 16
D_IDX = 64
CONV_W = 3
TOPK_MAX = 256
PAGE_SIZE = 128
ROPE_THETA = 10000.0
EPS = 1e-6
INDEX_SCALE = (D_IDX ** -0.5) * (N_IDX_HEADS ** -0.5)
ATT_SCALE = HEAD_DIM ** -0.5
EXP2_SCALE = ATT_SCALE * 1.4426950408889634

LANES = 128
SUBLANES = 8
VMEM_LIMIT = 56 * 1024 * 1024
INT_MIN = -2 ** 31
HALF16 = 2 ** 15
M_FLOOR = -1e30
ATT_BLOCK = 256
SAMPLE_ROWS = 16
ROW_TILE = 1024
ROW_TILE_FULL = 512
COL_TILE = 512


def _params(sem):
    return pltpu.CompilerParams(dimension_semantics=sem, vmem_limit_bytes=VMEM_LIMIT)


def _dot(a, b):
    return jnp.dot(a, b, preferred_element_type=F32)


def _dot_nt(a, b):
    return lax.dot_general(a, b, (((1,), (1,)), ((), ())), preferred_element_type=F32)


def _rmsnorm(x):
    return x * lax.rsqrt(jnp.mean(x * x, axis=-1, keepdims=True) + EPS)


def _sort_key(s):
    s = jnp.where(s == 0.0, 0.0, s)
    b = pltpu.bitcast(s, I32)
    return jnp.where(b < 0, b ^ 0x7FFFFFFF, b)


def _w_spec(k, tn, layer, col0=0, col_axis=None, row_axis=None):
    def index(*ids):
        return (layer, 0 if row_axis is None else ids[row_axis], col0 + (0 if col_axis is None else ids[col_axis]))
    return pl.BlockSpec((None, k, tn), index)


def _wt_spec(tn, k, layer, row0=0, row_axis=None):
    return pl.BlockSpec((None, tn, k), lambda *ids: (layer, row0 + (0 if row_axis is None else ids[row_axis]), 0))


def _wt_rows_spec(tn, k, layer, first_row, row_axis):
    assert first_row % SUBLANES == 0 and tn % SUBLANES == 0
    return pl.BlockSpec((pl.Element(1), pl.Element(tn), pl.Element(k)),
                        lambda *ids: (layer, pl.multiple_of(first_row + tn * ids[row_axis], SUBLANES), 0))


def _whole(a):
    return pl.BlockSpec(a.shape, lambda *ids: (0,) * a.ndim)


def _batch_vec(d, batch_axis):
    return pl.BlockSpec((None, 1, d), lambda *ids: (ids[batch_axis], 0, 0))


def _mod_kernel(c_ref, w_ref, b_ref, o_ref):
    c = c_ref[...]
    a = (c * jax.nn.sigmoid(c)).astype(BF16)
    o_ref[...] = _dot(a, w_ref[...].astype(BF16)) + b_ref[...]


def _mod_call(c_all, w_mod, b_mod):
    depth, d, n = w_mod.shape
    rows = c_all.shape[0]
    tn = 1024
    return pl.pallas_call(
        _mod_kernel,
        grid=(depth, n // tn),
        in_specs=[pl.BlockSpec((rows, d), lambda l, j: (0, 0)),
                  pl.BlockSpec((None, d, tn), lambda l, j: (l, 0, j)),
                  pl.BlockSpec((None, 1, tn), lambda l, j: (l, 0, j))],
        out_specs=pl.BlockSpec((None, rows, tn), lambda l, j: (l, 0, j)),
        out_shape=jax.ShapeDtypeStruct((depth, rows, n), F32),
        compiler_params=_params(("arbitrary", "arbitrary")),
        name="mod",
    )(c_all, w_mod, b_mod.reshape(depth, 1, n))


def _norm_mod(x, g, sc, sh):
    return (_rmsnorm(x) * g * (1.0 + sc) + sh).astype(BF16)


def _norm_mod_kernel(x_ref, xs_ref, g_ref, sc_ref, sh_ref, scs_ref, shs_ref, o_ref, os_ref):
    @pl.when((pl.program_id(0) == 0) & (pl.program_id(1) == 0))
    def _():
        os_ref[...] = _norm_mod(xs_ref[...], g_ref[...], scs_ref[...], shs_ref[...])

    o_ref[...] = _norm_mod(x_ref[...], g_ref[...], sc_ref[...], sh_ref[...])


def _norm_mod_call(x, xs, g, sc, sh, scs, shs, tm):
    gn, r, d = x.shape
    row = pl.BlockSpec((None, tm, d), lambda b, i: (b, i, 0))
    g = g.reshape(1, d)
    return pl.pallas_call(
        _norm_mod_kernel,
        grid=(gn, r // tm),
        in_specs=[row, _whole(xs), _whole(g), _batch_vec(d, 0), _batch_vec(d, 0), _whole(scs), _whole(shs)],
        out_specs=[row, _whole(xs)],
        out_shape=[jax.ShapeDtypeStruct((gn, r, d), BF16), jax.ShapeDtypeStruct(xs.shape, BF16)],
        compiler_params=_params(("arbitrary", "arbitrary")),
        name="norm_mod",
    )(x, xs, g, sc, sh, scs, shs)


def _conv_kernel(h_ref, hs_ref, wb_ref, wc_ref, wv_ref, cw_ref, prev_ref, p0_ref, p1_ref,
                 a_ref, st_ref, as_ref, us_ref, buf, w16, *, tm):
    b, i = pl.program_id(1), pl.program_id(2)
    cw = cw_ref[...]

    @pl.when((b == 0) & (i == 0))
    def _():
        for k, w_ref in enumerate((wb_ref, wc_ref, wv_ref)):
            w16[k] = w_ref[...].T.astype(BF16)
        hs = hs_ref[...]
        us = _dot(hs, w16[1]) * _dot(hs, w16[2])
        ys = cw[0:1] * p0_ref[...] + cw[1:2] * p1_ref[...] + cw[2:3] * us
        as_ref[...] = (_dot(hs, w16[0]) * ys).astype(BF16)
        us_ref[...] = us

    h = h_ref[...]
    bg = _dot(h, w16[0])
    u = _dot(h, w16[1]) * _dot(h, w16[2])

    @pl.when(i == 0)
    def _():
        buf[6:8, :] = prev_ref[...]

    buf[8:8 + tm, :] = u
    y = cw[0:1] * buf[6:6 + tm, :] + cw[1:2] * buf[7:7 + tm, :] + cw[2:3] * u
    a_ref[...] = (bg * y).astype(BF16)
    last2 = buf[6 + tm:8 + tm, :]
    buf[6:8, :] = last2
    st_ref[...] = last2


def _conv_call(h, hs, w_in, layer, cw, prev, p0, p1, tm, tn):
    gn, r, d = h.shape
    rs = hs.shape[0]
    n = cw.shape[1]
    nj = n // tn
    s_tile = pl.BlockSpec((rs, tn), lambda j, b, i: (0, j))
    return pl.pallas_call(
        functools.partial(_conv_kernel, tm=tm),
        grid=(nj, gn, r // tm),
        in_specs=[pl.BlockSpec((None, tm, d), lambda j, b, i: (b, i, 0)), _whole(hs),
                  _wt_spec(tn, d, layer, 0, 0), _wt_spec(tn, d, layer, nj, 0), _wt_spec(tn, d, layer, 2 * nj, 0),
                  pl.BlockSpec((CONV_W, tn), lambda j, b, i: (0, j)),
                  pl.BlockSpec((None, CONV_W - 1, tn), lambda j, b, i: (b, 0, j)),
                  s_tile, s_tile],
        out_specs=[pl.BlockSpec((None, tm, tn), lambda j, b, i: (b, i, j)),
                   pl.BlockSpec((None, CONV_W - 1, tn), lambda j, b, i: (b, 0, j)),
                   s_tile, s_tile],
        out_shape=[jax.ShapeDtypeStruct((gn, r, n), BF16), jax.ShapeDtypeStruct((gn, CONV_W - 1, n), F32),
                   jax.ShapeDtypeStruct((rs, n), BF16), jax.ShapeDtypeStruct((rs, n), F32)],
        scratch_shapes=[pltpu.VMEM((8 + tm, tn), F32), pltpu.VMEM((3, d, tn), BF16)],
        compiler_params=_params(("arbitrary", "arbitrary", "arbitrary")),
        name="conv",
    )(h, hs, w_in, w_in, w_in, cw, prev, p0, p1)


def _rope_group(x, cos, sin, half):
    if 2 * half == LANES:
        rot = pltpu.roll(x, half, 1)
    else:
        lane = lax.broadcasted_iota(I32, x.shape, 1)
        rot = jnp.where(lane % (2 * half) < half, pltpu.roll(x, LANES - half, 1), pltpu.roll(x, half, 1))
    return x * cos + rot * sin


def _proj_emit(acc, cos, sin, half, kinds, refs, tck):
    groups = []
    for c in range(acc.shape[1] // LANES):
        x = acc[:, c * LANES:(c + 1) * LANES]
        groups.append(_rope_group(x, cos, sin, half) if half else x)
    for kind, o in zip(kinds, refs):
        for c, x in enumerate(groups):
            if kind == "f32":
                o[:, c * LANES:(c + 1) * LANES] = x
            elif kind == "bf16":
                o[:, c * LANES:(c + 1) * LANES] = x.astype(BF16)
            elif kind == "heads":
                o[:, c, :] = x
            elif kind == "idx_heads":
                xb = x.astype(BF16)
                o[2 * c] = xb[:, :D_IDX]
                o[2 * c + 1] = xb[:, D_IDX:]
            elif kind == "chunk_t":
                for k in range(x.shape[0] // tck):
                    o[k, c * LANES:(c + 1) * LANES, :] = x[k * tck:(k + 1) * tck, :].T.astype(BF16)


def _proj_kernel(*refs, half, outs, outs_s, tck, aliased):
    h_ref, hs_ref, w_ref, cos_ref, sin_ref, cos_s_ref, sin_s_ref = refs[:7]
    rest = refs[7 + aliased:]
    out_refs, out_s_refs, w16 = rest[:len(outs)], rest[len(outs):len(outs) + len(outs_s)], rest[-1]

    @pl.when((pl.program_id(0) == 0) & (pl.program_id(1) == 0))
    def _():
        w16[...] = w_ref[...].T.astype(BF16)
        _proj_emit(_dot(hs_ref[...], w16[...]), cos_s_ref[...], sin_s_ref[...], half, outs_s, out_s_refs, None)

    _proj_emit(_dot(h_ref[...], w16[...]), cos_ref[...], sin_ref[...], half, outs, out_refs, tck)


def _proj_call(h, hs, w_in, layer, col0, n, rope, rope_s, tm, *, half, outs, outs_s, tck=None, cache=None):
    gn, r, d = h.shape
    rs = hs.shape[0]
    heads = n // LANES
    depth = None if cache is None else cache.shape[0]
    out_specs, out_shape = [], []
    for kind in outs:
        if kind == "bf16":
            out_specs.append(pl.BlockSpec((None, tm, n), lambda b, i: (b, i, 0)))
            out_shape.append(jax.ShapeDtypeStruct((gn, r, n), BF16))
        elif kind == "heads":
            out_specs.append(pl.BlockSpec((None, None, tm, heads, LANES), lambda b, i: (layer, b, i, 0, 0)))
            out_shape.append(jax.ShapeDtypeStruct((depth, gn, r, heads, LANES), F32))
        elif kind == "idx_heads":
            out_specs.append(pl.BlockSpec((None, n // D_IDX, tm, D_IDX), lambda b, i: (b, 0, i, 0)))
            out_shape.append(jax.ShapeDtypeStruct((gn, n // D_IDX, r, D_IDX), BF16))
        elif kind == "chunk_t":
            out_specs.append(pl.BlockSpec((None, tm // tck, n, tck), lambda b, i: (b, i, 0, 0)))
            out_shape.append(jax.ShapeDtypeStruct((gn, r // tck, n, tck), BF16))
    for kind in outs_s:
        shape, dtype = ((rs, n), F32) if kind == "f32" else ((n // D_IDX, rs, D_IDX), BF16)
        out_specs.append(pl.BlockSpec(shape, lambda b, i, nd=len(shape): (0,) * nd))
        out_shape.append(jax.ShapeDtypeStruct(shape, dtype))
    tab = pl.BlockSpec((tm, LANES), lambda b, i: (i, 0))
    in_specs = [pl.BlockSpec((None, tm, d), lambda b, i: (b, i, 0)), _whole(hs), _wt_spec(n, d, layer, col0),
                tab, tab, _whole(rope_s[0]), _whole(rope_s[1])]
    args = [h, hs, w_in, rope[0], rope[1], rope_s[0], rope_s[1]]
    aliases = {}
    if cache is not None:
        in_specs.append(pl.BlockSpec(memory_space=pl.ANY))
        args.append(cache)
        aliases = {7: outs.index("heads")}
    return pl.pallas_call(
        functools.partial(_proj_kernel, half=half, outs=outs, outs_s=outs_s, tck=tck, aliased=cache is not None),
        grid=(gn, r // tm),
        in_specs=in_specs, out_specs=out_specs, out_shape=out_shape,
        scratch_shapes=[pltpu.VMEM((d, n), BF16)],
        input_output_aliases=aliases,
        compiler_params=_params(("arbitrary", "arbitrary")),
        name="proj",
    )(*args)


def _kidx_rows(acc, cos, sin):
    roped = _rope_group(acc, cos, sin, D_IDX // 2)
    lane = lax.broadcasted_iota(I32, acc.shape, 1)
    return jnp.where(lane < D_IDX, roped, jnp.where(lane < D_IDX + N_IDX_HEADS, acc, 0.0))


def _kidx_kernel(h_ref, hs_ref, w_ref, cos_ref, sin_ref, cos_s_ref, sin_s_ref,
                 kw_ref, kb_ref, wt_ref, kws_ref, kbs_ref, w16):
    @pl.when((pl.program_id(0) == 0) & (pl.program_id(1) == 0))
    def _():
        w16[...] = w_ref[...].T.astype(BF16)
        kws = _kidx_rows(_dot(hs_ref[...], w16[...]), cos_s_ref[...], sin_s_ref[...])
        kws_ref[...] = kws
        kbs_ref[...] = kws[:, :D_IDX].astype(BF16)

    kw = _kidx_rows(_dot(h_ref[...], w16[...]), cos_ref[...], sin_ref[...])
    kw_ref[...] = kw
    kb_ref[...] = kw[:, :D_IDX].astype(BF16)
    wt_ref[...] = kw.T[D_IDX:D_IDX + N_IDX_HEADS, :]


def _kidx_call(h, hs, w_in, layer, col0, rope, rope_s, tm):
    gn, r, d = h.shape
    rs = hs.shape[0]
    tab = pl.BlockSpec((tm, LANES), lambda b, i: (i, 0))
    s128 = pl.BlockSpec((rs, LANES), lambda b, i: (0, 0))
    s64 = pl.BlockSpec((rs, D_IDX), lambda b, i: (0, 0))
    return pl.pallas_call(
        _kidx_kernel,
        grid=(gn, r // tm),
        in_specs=[pl.BlockSpec((None, tm, d), lambda b, i: (b, i, 0)), _whole(hs), _wt_spec(LANES, d, layer, col0),
                  tab, tab, s128, s128],
        out_specs=[pl.BlockSpec((None, tm, LANES), lambda b, i: (b, i, 0)),
                   pl.BlockSpec((None, tm, D_IDX), lambda b, i: (b, i, 0)),
                   pl.BlockSpec((None, N_IDX_HEADS, tm), lambda b, i: (b, 0, i)),
                   s128, s64],
        out_shape=[jax.ShapeDtypeStruct((gn, r, LANES), F32), jax.ShapeDtypeStruct((gn, r, D_IDX), BF16),
                   jax.ShapeDtypeStruct((gn, N_IDX_HEADS, r), F32),
                   jax.ShapeDtypeStruct((rs, LANES), F32), jax.ShapeDtypeStruct((rs, D_IDX), BF16)],
        scratch_shapes=[pltpu.VMEM((d, LANES), BF16)],
        compiler_params=_params(("arbitrary", "arbitrary")),
        name="kidx",
    )(h, hs, w_in, rope[0], rope[1], rope_s[0], rope_s[1])


def _strict_tri(n, dtype, lower):
    row = lax.broadcasted_iota(I32, (n, n), 0)
    col = lax.broadcasted_iota(I32, (n, n), 1)
    return jnp.where(col < row if lower else row < col, 1.0, 0.0).astype(dtype)


def _dsa_kernel(q_ref, qi_ref, wi_ref, k_ref, vt_ref, ki_ref, o_ref,
                key_scr, hi_scr, lo_scr, m_scr, l_scr, acc_scr, *, topk, ck, qb):
    i = pl.program_id(1)
    nch = ((i + 1) * qb + ck - 1) // ck
    qpos = i * qb + lax.broadcasted_iota(I32, (LANES, qb), 1)
    krow = lax.broadcasted_iota(I32, (LANES, qb), 0)
    w = wi_ref[...]

    def index_chunk(c, _):
        base = pl.multiple_of(c * ck, ck)
        for r in range(ck // LANES):
            rows = slice(r * LANES, (r + 1) * LANES)
            kc = ki_ref[pl.ds(base + r * LANES, LANES), :]
            acc = jnp.zeros((LANES, qb), F32)
            for h in range(N_IDX_HEADS):
                acc = acc + jnp.maximum(_dot_nt(kc, qi_ref[h]), 0.0) * w[h:h + 1, :]
            kpos = base + r * LANES + krow
            key = jnp.where(kpos <= qpos, _sort_key(acc * INDEX_SCALE), INT_MIN)
            key_scr[c, rows, :] = key
            hi_scr[c, rows, :] = lax.shift_right_arithmetic(key, 16).astype(I16)
            lo_scr[c, rows, :] = ((key & 0xFFFF) - HALF16).astype(I16)
        return 0

    lax.fori_loop(0, nch, index_chunk, 0)

    ge = lambda a, b: a >= b
    gt = lambda a, b: a > b
    pack = 2 * SUBLANES

    def count16(scr, pred, thr):
        thr = thr.astype(I16)

        def body(c, part):
            hit = jnp.where(pred(scr[c], thr), jnp.ones((), BF16), jnp.zeros((), BF16))
            terms = [hit[k * pack:(k + 1) * pack] for k in range(ck // pack)]
            while len(terms) > 1:
                terms = [a + b for a, b in zip(terms[0::2], terms[1::2])]
            return part + terms[0]

        part = lax.fori_loop(0, nch, body, jnp.zeros((pack, qb), BF16))
        return jnp.sum(part.astype(F32), axis=0, keepdims=True).astype(I32)

    def search16(scr, need):
        def step(it, t):
            cand = t + jnp.left_shift(jnp.int32(1), 15 - it)
            return jnp.where(count16(scr, ge, cand) >= need, cand, t)
        return lax.fori_loop(0, 16, step, jnp.full((1, qb), -HALF16, I32))

    t_hi = search16(hi_scr, topk)
    need_lo = topk - count16(hi_scr, gt, t_hi)
    t_hi16 = t_hi.astype(I16)

    def keep_band(c, _):
        lo_scr[c] = jnp.where(hi_scr[c] == t_hi16, lo_scr[c], jnp.full((), -HALF16, I16))
        return 0

    lax.fori_loop(0, nch, keep_band, 0)
    t_lo = search16(lo_scr, need_lo)
    t = t_hi * (2 * HALF16) + (t_lo + HALF16)
    t = jnp.maximum(t, INT_MIN + 1)

    def count(pred, thr):
        def body(c, cnt):
            hit = jnp.where(pred(key_scr[c], thr), 1, 0)
            return cnt + jnp.sum(hit.reshape(ck // SUBLANES, SUBLANES, qb), axis=0)
        cnt = lax.fori_loop(0, nch, body, jnp.zeros((SUBLANES, qb), I32))
        return jnp.sum(cnt, axis=0, keepdims=True)

    c_ge = count(ge, t)

    @pl.when(jnp.max(c_ge) > topk)
    def _():
        need = (topk - count(gt, t)).astype(F32)
        lower = _strict_tri(ck, BF16, lower=True)

        def body(c, carry):
            kc = key_scr[c]
            eq = kc == t
            eqf = jnp.where(eq, 1.0, 0.0)
            before = _dot(lower, eqf.astype(BF16)) + carry
            key_scr[c] = jnp.where(eq & (before >= need), INT_MIN, kc)
            return carry + jnp.sum(eqf, axis=0, keepdims=True)

        lax.fori_loop(0, nch, body, jnp.zeros((1, qb), F32))

    m_scr[...] = jnp.full(m_scr.shape, M_FLOOR, F32)
    l_scr[...] = jnp.zeros(l_scr.shape, F32)
    acc_scr[...] = jnp.zeros(acc_scr.shape, F32)

    heads = [slice(h * HEAD_DIM, (h + 1) * HEAD_DIM) for h in range(N_HEADS)]

    def attend(chunks):
        bases = [pl.multiple_of(c * ck, ck) for c in chunks]
        sels = [key_scr[c] >= t for c in chunks]
        raw = [[_dot_nt(k_ref[pl.ds(b, ck), cols], q_ref[:, cols]) for cols in heads] for b in bases]
        probs = []
        for h in range(N_HEADS):
            ss = [jnp.where(sel, r[h], -jnp.inf) for sel, r in zip(sels, raw)]
            m_old = m_scr[h]
            m_new = m_old
            for s in ss:
                m_new = jnp.maximum(m_new, jnp.max(s, axis=0, keepdims=True))
            alpha = jnp.exp2((m_old - m_new) * EXP2_SCALE)
            ps = [jnp.exp2((s - m_new) * EXP2_SCALE) for s in ss]
            l_new = alpha * l_scr[h]
            for p in ps:
                l_new = l_new + jnp.sum(p, axis=0, keepdims=True)
            l_scr[h] = l_new
            m_scr[h] = m_new
            probs.append((alpha, [p.astype(BF16) for p in ps]))
        for h in range(N_HEADS):
            alpha, ps = probs[h]
            upd = alpha * acc_scr[h]
            for c, p in zip(chunks, ps):
                upd = upd + _dot(vt_ref[c, heads[h], :], p)
            acc_scr[h] = upd

    def attend_pair(j, _):
        attend([2 * j, 2 * j + 1])
        return 0

    lax.fori_loop(0, nch // 2, attend_pair, 0)

    @pl.when(nch % 2 == 1)
    def _():
        attend([nch - 1])

    for h in range(N_HEADS):
        o_ref[:, h * HEAD_DIM:(h + 1) * HEAD_DIM] = (acc_scr[h] / l_scr[h]).T.astype(o_ref.dtype)


def _dsa_call(q, qi_hm, wi_t, k, v_t, ki, topk, blk):
    bn, t, a = q.shape
    assert t <= 256 * 2 * SUBLANES, "the threshold search counts in bf16: at most 256 keys per packed-row entry"
    full = lambda shape: pl.BlockSpec((None,) + shape, lambda b, i: (b,) + (0,) * len(shape))
    return pl.pallas_call(
        functools.partial(_dsa_kernel, topk=topk, ck=blk, qb=blk),
        grid=(bn, t // blk),
        in_specs=[pl.BlockSpec((None, blk, a), lambda b, i: (b, i, 0)),
                  pl.BlockSpec((None, N_IDX_HEADS, blk, D_IDX), lambda b, i: (b, 0, i, 0)),
                  pl.BlockSpec((None, N_IDX_HEADS, blk), lambda b, i: (b, 0, i)),
                  full((t, a)), full((t // blk, a, blk)), full((t, D_IDX))],
        out_specs=pl.BlockSpec((None, blk, a), lambda b, i: (b, i, 0)),
        out_shape=jax.ShapeDtypeStruct((bn, t, a), BF16),
        scratch_shapes=[pltpu.VMEM((t // blk, blk, blk), I32),
                        pltpu.VMEM((t // blk, blk, blk), I16),
                        pltpu.VMEM((t // blk, blk, blk), I16),
                        pltpu.VMEM((N_HEADS, 1, blk), F32),
                        pltpu.VMEM((N_HEADS, 1, blk), F32),
                        pltpu.VMEM((N_HEADS, HEAD_DIM, blk), F32)],
        compiler_params=_params(("arbitrary", "arbitrary")),
        name="dsa_prompt",
    )(q, qi_hm, wi_t, k, v_t, ki)


def _pool_scores_kernel(qi_ref, w_ref, ck_ref, o_ref, *, nb, group):
    pages, _, page = ck_ref.shape
    qi, w = qi_ref[...], w_ref[...]
    for p in range(0, pages, group):
        kc = jnp.concatenate([ck_ref[p + g] for g in range(group)], axis=1).astype(BF16)
        d = jnp.maximum(_dot(qi, kc), 0.0) * w
        s = d[0:nb]
        for h in range(1, N_IDX_HEADS):
            s = s + d[h * nb:(h + 1) * nb]
        o_ref[:, p * page:(p + group) * page] = s * INDEX_SCALE


def _pool_scores_call(qi_hb, w_hb, kidx_t, layer, nb, pages_per_step):
    depth, n_pool, di, page = kidx_t.shape
    group = 2 if pages_per_step % 2 == 0 else 1
    return pl.pallas_call(
        functools.partial(_pool_scores_kernel, nb=nb, group=group),
        grid=(n_pool // pages_per_step,),
        in_specs=[pl.BlockSpec(qi_hb.shape, lambda j: (0, 0)),
                  pl.BlockSpec(w_hb.shape, lambda j: (0, 0)),
                  pl.BlockSpec((None, pages_per_step, di, page), lambda j: (layer, j, 0, 0))],
        out_specs=pl.BlockSpec((nb, pages_per_step * page), lambda j: (0, j)),
        out_shape=jax.ShapeDtypeStruct((nb, n_pool * page), F32),
        compiler_params=_params(("arbitrary",)),
        name="pool_scores",
    )(qi_hb, w_hb, kidx_t)


def _select_kernel(pt_ref, sc_ref, ptv_ref, qi_ref, w_ref, kn_ref, row_ref, s_scr, *, topk, n_pages, nb):
    def gather_page(p, _):
        for b in range(nb):
            s_scr[b, pl.ds(p, 1), :] = sc_ref[b, pl.ds(pt_ref[b, p], 1), :]
        return 0

    lax.fori_loop(0, n_pages, gather_page, 0)
    key = _sort_key(s_scr[...])

    def total(x):
        return jnp.sum(jnp.sum(x, axis=1, keepdims=True), axis=2, keepdims=True)

    d = jnp.maximum(_dot_nt(qi_ref[...], kn_ref[...]), 0.0) * w_ref[...]
    row = lax.broadcasted_iota(I32, d.shape, 0)
    col = lax.broadcasted_iota(I32, d.shape, 1)
    s_new = jnp.stack([jnp.sum(jnp.where((row % nb == b) & (col == b), d, 0.0), keepdims=True) for b in range(nb)])
    key_new = _sort_key(s_new * INDEX_SCALE)

    def search(it, t):
        cand = t + jnp.left_shift(jnp.int32(1), 31 - it)
        n_ge = total(jnp.where(key >= cand, 1, 0)) + jnp.where(key_new >= cand, 1, 0)
        return jnp.where(n_ge >= topk, cand, t)

    t = lax.fori_loop(0, 32, search, jnp.full((nb, 1, 1), INT_MIN, I32))
    t = jnp.maximum(t, INT_MIN + 1)

    upper = _strict_tri(PAGE_SIZE, BF16, lower=False)
    lower = _strict_tri(n_pages, BF16, lower=True)
    ones = jnp.ones((SUBLANES, PAGE_SIZE), BF16)
    incl = jnp.where(lax.broadcasted_iota(I32, (n_pages, n_pages), 0) <= lax.broadcasted_iota(I32, (n_pages, n_pages), 1),
                     1.0, 0.0).astype(BF16)
    slot = lax.broadcasted_iota(I32, (topk, PAGE_SIZE), 0).astype(F32)
    lane = lax.broadcasted_iota(I32, (topk, PAGE_SIZE), 1)
    page_lane = lax.broadcasted_iota(I32, (topk, n_pages), 1)

    def ranks(f16):
        per_page = jnp.broadcast_to(jnp.sum(f16.astype(F32), axis=1, keepdims=True), f16.shape)
        return _dot(f16, upper), _dot(lower, per_page.astype(BF16))

    for b in range(nb):
        kb, tb, knb = key[b], t[b], key_new[b]
        n_gt = jnp.sum(jnp.where(kb > tb, 1.0, 0.0), keepdims=True) + jnp.where(knb > tb, 1.0, 0.0)
        need = topk - n_gt
        eq = kb == tb
        eq16 = jnp.where(eq, 1.0, 0.0).astype(BF16)
        eq_in, eq_before = ranks(eq16)
        sel = (kb > tb) | (eq & (eq_in + eq_before < need))
        sel_new = (knb > tb) | ((knb == tb) & (jnp.sum(eq16.astype(F32), keepdims=True) < need))

        sel16 = jnp.where(sel, 1.0, 0.0).astype(BF16)
        in_page, before_page = ranks(sel16)
        n_sel = jnp.sum(sel16.astype(F32), keepdims=True)
        cum = _dot(_dot_nt(ones, sel16).astype(BF16), incl)[0:1, :]
        page_r = jnp.sum(jnp.where(cum <= slot[:, 0:1], 1.0, 0.0), axis=1, keepdims=True).astype(I32)
        hit = page_lane == page_r
        onehot = jnp.where(hit, 1.0, 0.0).astype(BF16)
        got = _dot(onehot, jnp.where(sel, in_page, -1.0).astype(BF16))
        within = slot - _dot(onehot, before_page.astype(BF16))
        off = jnp.sum(jnp.where(got == within, lane, 0), axis=1, keepdims=True)
        phys = jnp.sum(jnp.where(hit, ptv_ref[b:b + 1, :], 0), axis=1, keepdims=True)
        rows = jnp.where(slot[:, 0:1] < n_sel, phys * PAGE_SIZE + off, 0)
        row_ref[b] = rows - jnp.where(sel_new & (slot[:, 0:1] == n_sel), 1, 0)


def _select_call(page_table, scores, qi_hb, w_hb, k_new, topk):
    nb, n_pages = page_table.shape
    whole = lambda a: pl.BlockSpec(a.shape, lambda i, pt: (0,) * a.ndim)
    return pl.pallas_call(
        functools.partial(_select_kernel, topk=topk, n_pages=n_pages, nb=nb),
        grid_spec=pltpu.PrefetchScalarGridSpec(
            num_scalar_prefetch=1, grid=(1,),
            in_specs=[whole(scores), whole(page_table), whole(qi_hb), whole(w_hb), whole(k_new)],
            out_specs=pl.BlockSpec((nb, topk, 1), lambda i, pt: (0, 0, 0)),
            scratch_shapes=[pltpu.VMEM((nb, n_pages, PAGE_SIZE), F32)]),
        out_shape=jax.ShapeDtypeStruct((nb, topk, 1), I32),
        compiler_params=_params(("arbitrary",)),
        name="select",
    )(page_table, scores, page_table, qi_hb, w_hb, k_new)


def _gather_attend_kernel(row_ref, q_ref, kn_ref, vn_ref, ck_hbm, cv_hbm, o_ref, kbuf, vbuf, sem, *, layer, topk):
    b = pl.program_id(0)

    def copies(r):
        src = jnp.maximum(row_ref[b, r], 0)
        return (pltpu.make_async_copy(ck_hbm.at[layer, src], kbuf.at[r], sem.at[0]),
                pltpu.make_async_copy(cv_hbm.at[layer, src], vbuf.at[r], sem.at[1]))

    def start(r, _):
        for cp in copies(r):
            cp.start()
        return 0

    def wait(r, _):
        for cp in copies(r):
            cp.wait()
        return 0

    lax.fori_loop(0, topk, start, 0)
    lax.fori_loop(0, topk, wait, 0)

    @pl.when(row_ref[b, topk - 1] < 0)
    def _():
        kbuf[topk - 1] = kn_ref[...]
        vbuf[topk - 1] = vn_ref[...]

    q = q_ref[...]
    s = jnp.sum(kbuf[...] * q[None], axis=-1, keepdims=True) * ATT_SCALE
    p = jnp.exp(s - jnp.max(s, axis=0, keepdims=True))
    p = p / jnp.sum(p, axis=0, keepdims=True)
    o_ref[...] = jnp.sum(p * vbuf[...], axis=0)


def _gather_attend_call(rows, q, k_new, v_new, cache_k, cache_v, layer, topk):
    nb = rows.shape[0]
    any_spec = pl.BlockSpec(memory_space=pl.ANY)
    per_sample = pl.BlockSpec((None, N_HEADS, HEAD_DIM), lambda b, rows: (b, 0, 0))
    return pl.pallas_call(
        functools.partial(_gather_attend_kernel, layer=layer, topk=topk),
        grid_spec=pltpu.PrefetchScalarGridSpec(
            num_scalar_prefetch=1, grid=(nb,),
            in_specs=[per_sample, per_sample, per_sample, any_spec, any_spec],
            out_specs=per_sample,
            scratch_shapes=[pltpu.VMEM((topk, N_HEADS, HEAD_DIM), F32),
                            pltpu.VMEM((topk, N_HEADS, HEAD_DIM), F32),
                            pltpu.SemaphoreType.DMA((2,))]),
        out_shape=jax.ShapeDtypeStruct((nb, N_HEADS, HEAD_DIM), F32),
        compiler_params=_params(("arbitrary",)),
        name="gather_attend",
    )(rows, q, k_new, v_new, cache_k, cache_v)


def _merge_kernel(h_ref, a_ref, t_ref, hs_ref, as_ref, ts_ref, wga_ref, wgb_ref, wpa_ref, wpb_ref,
                  o_ref, os_ref, wga16, wgb16, wpa16, wpb16):
    def mix(h, a, t):
        ya = _dot(a, wpa16[...])
        yb = _dot(t, wpb16[...])
        ga = jax.nn.sigmoid(_dot(h, wga16[...]))
        gb = jax.nn.sigmoid(_dot(h, wgb16[...]))
        return (ga * ya + gb * yb).astype(BF16)

    @pl.when((pl.program_id(1) == 0) & (pl.program_id(2) == 0))
    def _():
        wga16[...] = wga_ref[0].T.astype(BF16)
        wgb16[...] = wgb_ref[0].T.astype(BF16)
        wpa16[...] = wpa_ref[...].astype(BF16)
        wpb16[...] = wpb_ref[...].astype(BF16)
        os_ref[...] = mix(hs_ref[...], as_ref[...], ts_ref[...])

    o_ref[...] = mix(h_ref[...], a_ref[...], t_ref[...])


def _merge_call(h, a_in, att, hs, as_in, att_s, w_in, gate_row, w_pa, w_pb, layer, tm, tn):
    gn, r, d = h.shape
    a = a_in.shape[2]
    rs = hs.shape[0]
    nj = d // tn
    rows = lambda w: pl.BlockSpec((None, tm, w), lambda j, b, i: (b, i, 0))
    s_out = pl.BlockSpec((rs, tn), lambda j, b, i: (0, j))
    return pl.pallas_call(
        _merge_kernel,
        grid=(nj, gn, r // tm),
        in_specs=[rows(d), rows(a), rows(a), _whole(hs), _whole(as_in), _whole(att_s),
                  _wt_rows_spec(tn, d, layer, gate_row, 0), _wt_rows_spec(tn, d, layer, gate_row + d, 0),
                  _w_spec(a, tn, layer, 0, 0), _w_spec(a, tn, layer, 0, 0)],
        out_specs=[pl.BlockSpec((None, tm, tn), lambda j, b, i: (b, i, j)), s_out],
        out_shape=[jax.ShapeDtypeStruct((gn, r, d), BF16), jax.ShapeDtypeStruct((rs, d), BF16)],
        scratch_shapes=[pltpu.VMEM((d, tn), BF16), pltpu.VMEM((d, tn), BF16),
                        pltpu.VMEM((a, tn), BF16), pltpu.VMEM((a, tn), BF16)],
        compiler_params=_params(("arbitrary", "arbitrary", "arbitrary")),
        name="merge",
    )(h, a_in, att, hs, as_in, att_s, w_in, w_in, w_pa, w_pb)


def _resid_norm_kernel(x_ref, m_ref, xs_ref, ms_ref, w_ref, g_ref, gt_ref, sc_ref, sh_ref, gts_ref, scs_ref, shs_ref,
                       x_out, h_out, xs_out, hs_out):
    def step(x, m, gt, sc, sh):
        x = x + gt * _dot(m, w_ref[...])
        return x, _norm_mod(x, g_ref[...], sc, sh)

    @pl.when((pl.program_id(0) == 0) & (pl.program_id(1) == 0))
    def _():
        xs_out[...], hs_out[...] = step(xs_ref[...], ms_ref[...], gts_ref[...], scs_ref[...], shs_ref[...])

    x_out[...], h_out[...] = step(x_ref[...], m_ref[...], gt_ref[...], sc_ref[...], sh_ref[...])


def _resid_norm_call(x, m, xs, ms, w_o, layer, g, gt, sc, sh, gts, scs, shs, tm):
    gn, r, d = x.shape
    row = pl.BlockSpec((None, tm, d), lambda b, i: (b, i, 0))
    g = g.reshape(1, d)
    return pl.pallas_call(
        _resid_norm_kernel,
        grid=(gn, r // tm),
        in_specs=[row, row, _whole(xs), _whole(ms), _w_spec(d, d, layer), _whole(g),
                  _batch_vec(d, 0), _batch_vec(d, 0), _batch_vec(d, 0), _whole(gts), _whole(scs), _whole(shs)],
        out_specs=[row, row, _whole(xs), _whole(xs)],
        out_shape=[jax.ShapeDtypeStruct((gn, r, d), F32), jax.ShapeDtypeStruct((gn, r, d), BF16),
                   jax.ShapeDtypeStruct(xs.shape, F32), jax.ShapeDtypeStruct(xs.shape, BF16)],
        compiler_params=_params(("arbitrary", "arbitrary")),
        name="resid_norm",
    )(x, m, xs, ms, w_o, g, gt, sc, sh, gts, scs, shs)


def _ffn_up_kernel(h_ref, hs_ref, wg_ref, wu_ref, o_ref, os_ref, wg16, wu16):
    def act(h):
        g = _dot(h, wg16[...])
        return (g * jax.nn.sigmoid(g) * _dot(h, wu16[...])).astype(BF16)

    @pl.when((pl.program_id(1) == 0) & (pl.program_id(2) == 0))
    def _():
        wg16[...] = wg_ref[...].astype(BF16)
        wu16[...] = wu_ref[...].astype(BF16)
        os_ref[...] = act(hs_ref[...])

    o_ref[...] = act(h_ref[...])


def _ffn_up_call(h, hs, w_gate, w_up, layer, tm, tn):
    gn, r, d = h.shape
    rs = hs.shape[0]
    f = w_gate.shape[2]
    return pl.pallas_call(
        _ffn_up_kernel,
        grid=(f // tn, gn, r // tm),
        in_specs=[pl.BlockSpec((None, tm, d), lambda j, b, i: (b, i, 0)), _whole(hs),
                  _w_spec(d, tn, layer, 0, 0), _w_spec(d, tn, layer, 0, 0)],
        out_specs=[pl.BlockSpec((None, tm, tn), lambda j, b, i: (b, i, j)),
                   pl.BlockSpec((rs, tn), lambda j, b, i: (0, j))],
        out_shape=[jax.ShapeDtypeStruct((gn, r, f), BF16), jax.ShapeDtypeStruct((rs, f), BF16)],
        scratch_shapes=[pltpu.VMEM((d, tn), BF16), pltpu.VMEM((d, tn), BF16)],
        compiler_params=_params(("arbitrary", "arbitrary", "arbitrary")),
        name="ffn_up",
    )(h, hs, w_gate, w_up)


def _ffn_down_kernel(x_ref, a_ref, xs_ref, as_ref, w_ref, g_ref, gt_ref, sc_ref, sh_ref, gts_ref, scs_ref, shs_ref,
                     x_out, n_out, xs_out, ns_out, acc, acc_s, *, final):
    k = pl.program_id(2)
    last = k == pl.num_programs(2) - 1
    with_samples = (pl.program_id(0) == 0) & (pl.program_id(1) == 0)

    def finish(x, gt, sc, sh, total, x_o, n_o):
        x = x + gt * total
        x_o[...] = x
        n_o[...] = (_rmsnorm(x) * g_ref[...]).astype(n_o.dtype) if final else _norm_mod(x, g_ref[...], sc, sh)

    @pl.when(k == 0)
    def _():
        acc[...] = jnp.zeros(acc.shape, F32)

    acc[...] += _dot(a_ref[...], w_ref[...])

    @pl.when(with_samples & (k == 0))
    def _():
        acc_s[...] = jnp.zeros(acc_s.shape, F32)

    @pl.when(with_samples)
    def _():
        acc_s[...] += _dot(as_ref[...], w_ref[...])

    @pl.when(with_samples & last)
    def _():
        finish(xs_ref[...], gts_ref[...], scs_ref[...], shs_ref[...], acc_s[...], xs_out, ns_out)

    @pl.when(last)
    def _():
        finish(x_ref[...], gt_ref[...], sc_ref[...], sh_ref[...], acc[...], x_out, n_out)


def _ffn_down_call(x, act, xs, act_s, w_down, layer, g, gt, sc, sh, gts, scs, shs, tm, tk, final):
    gn, r, d = x.shape
    rs = xs.shape[0]
    f = act.shape[2]
    row = pl.BlockSpec((None, tm, d), lambda b, i, k: (b, i, 0))
    g = g.reshape(1, d)
    n_dtype = F32 if final else BF16
    return pl.pallas_call(
        functools.partial(_ffn_down_kernel, final=final),
        grid=(gn, r // tm, f // tk),
        in_specs=[row, pl.BlockSpec((None, tm, tk), lambda b, i, k: (b, i, k)),
                  _whole(xs), pl.BlockSpec((rs, tk), lambda b, i, k: (0, k)),
                  _w_spec(tk, d, layer, row_axis=2), _whole(g),
                  _batch_vec(d, 0), _batch_vec(d, 0), _batch_vec(d, 0), _whole(gts), _whole(scs), _whole(shs)],
        out_specs=[row, row, _whole(xs), _whole(xs)],
        out_shape=[jax.ShapeDtypeStruct((gn, r, d), F32), jax.ShapeDtypeStruct((gn, r, d), n_dtype),
                   jax.ShapeDtypeStruct(xs.shape, F32), jax.ShapeDtypeStruct(xs.shape, n_dtype)],
        scratch_shapes=[pltpu.VMEM((tm, d), F32), pltpu.VMEM((rs, d), F32)],
        compiler_params=_params(("arbitrary", "arbitrary", "arbitrary")),
        name="ffn_down",
    )(x, act, xs, act_s, w_down, g, gt, sc, sh, gts, scs, shs)


def _rope_tables(pos, d):
    inv_freq = ROPE_THETA ** (-np.arange(0, d, 2, dtype=np.float64) / d)
    ang = np.asarray(pos, np.float64)[:, None] * inv_freq[None, :]
    cos, sin = np.cos(ang), np.sin(ang)
    reps = LANES // d
    return (jnp.asarray(np.tile(np.concatenate([cos, cos], axis=1), (1, reps)), F32),
            jnp.asarray(np.tile(np.concatenate([-sin, sin], axis=1), (1, reps)), F32))


def _pages_per_step(n_pool):
    return max(p for p in range(1, 33) if n_pool % p == 0)


def kernel(x_prompt, x_sample, c_prompt, c_sample, cache_k, cache_v, cache_kidx, state_conv, page_table,
           w_mod, b_mod, g_mix, g_ffn, w_in, conv_w, w_pa, w_pb, w_o, w_gate, w_up, w_down, g_final):
    bp, tp, d = x_prompt.shape
    db, ts, _ = x_sample.shape
    assert ts == 1, "the sample path handles one new token per sequence"
    depth = w_mod.shape[0]
    n_pages = page_table.shape[1]
    past = n_pages * PAGE_SIZE
    att_dim = N_HEADS * HEAD_DIM
    d_conv = d // 2
    idx_q = N_IDX_HEADS * D_IDX
    topk_p = min(TOPK_MAX, tp // 4)
    topk_s = min(TOPK_MAX, (past + ts) // 4)
    rs = SAMPLE_ROWS
    blk = ATT_BLOCK if tp % ATT_BLOCK == 0 else LANES
    tm, tm_full, tn = min(tp, ROW_TILE), min(tp, ROW_TILE_FULL), COL_TILE
    assert db <= rs and tp % blk == 0 and tp % tm == 0 and d_conv == att_dim == idx_q

    cb_q, cb_k, cb_v, cb_qi = 3, 4, 5, 6
    o_ki = 7 * att_dim
    o_ga = o_ki + D_IDX + N_IDX_HEADS
    assert o_ki % LANES == 0

    n_pool = cache_kidx.shape[1]
    kidx_t = jnp.swapaxes(cache_kidx, 2, 3)
    ck_rows = cache_k.reshape(depth, n_pool * PAGE_SIZE, N_HEADS, HEAD_DIM)
    cv_rows = cache_v.reshape(depth, n_pool * PAGE_SIZE, N_HEADS, HEAD_DIM)

    w_in_t = jnp.swapaxes(w_in, 1, 2)
    w_o16, w_down16 = w_o.astype(BF16), w_down.astype(BF16)

    c_all = jnp.zeros((rs, d), F32).at[:bp].set(c_prompt).at[bp:bp + db].set(c_sample)
    mod = _mod_call(c_all, w_mod, b_mod)
    mod_p = mod[:, :bp].reshape(depth, bp, 1, 6, d)
    mod_s = jnp.zeros((depth, rs, 6, d), F32).at[:, :db].set(mod[:, bp:bp + db].reshape(depth, db, 6, d))

    rope_p, ropei_p = _rope_tables(np.arange(tp), HEAD_DIM), _rope_tables(np.arange(tp), D_IDX)
    rope_s, ropei_s = _rope_tables(np.full((rs,), past), HEAD_DIM), _rope_tables(np.full((rs,), past), D_IDX)

    xp = x_prompt
    xs = jnp.zeros((rs, d), F32).at[:db].set(x_sample[:, 0])
    tk_ffn = w_down.shape[1] // 4

    hp, hs = _norm_mod_call(xp, xs, g_mix[0], mod_p[0, :, :, 1], mod_p[0, :, :, 0], mod_s[0, :, 1], mod_s[0, :, 0], tm_full)

    outs = {n: [] for n in ("kip", "cp", "ks", "vs", "kis", "cs")}
    kp_all = jnp.zeros((depth, bp, tp, N_HEADS, HEAD_DIM), F32)
    vp_all = jnp.zeros((depth, bp, tp, N_HEADS, HEAD_DIM), F32)
    yp = ys = None
    for l in range(depth):
        last = l == depth - 1
        g_next = g_final if last else g_mix[l + 1]
        ln = l if last else l + 1
        sh1, sc1, gt1, sh2, sc2, gt2 = (mod_p[l, :, :, j] for j in range(6))
        sh1s, sc1s, gt1s, sh2s, sc2s, gt2s = (mod_s[l, :, j] for j in range(6))
        prev_s = jnp.zeros((rs, CONV_W - 1, d_conv), F32).at[:db].set(state_conv[l])

        a_in, conv_st, a_in_s, u_s = _conv_call(hp, hs, w_in_t, l, conv_w[l], jnp.zeros((bp, CONV_W - 1, d_conv), F32),
                                                prev_s[:, 0], prev_s[:, 1], tm, tn)
        q, q_s = _proj_call(hp, hs, w_in_t, l, cb_q, att_dim, rope_p, rope_s, tm, half=HEAD_DIM // 2,
                            outs=("bf16",), outs_s=("f32",))
        kp_all, k16, k_s = _proj_call(hp, hs, w_in_t, l, cb_k, att_dim, rope_p, rope_s, tm, half=HEAD_DIM // 2,
                                      outs=("heads", "bf16"), outs_s=("f32",), cache=kp_all)
        vp_all, v_t, v_s = _proj_call(hp, hs, w_in_t, l, cb_v, att_dim, rope_p, rope_s, tm, half=0,
                                      outs=("heads", "chunk_t"), outs_s=("f32",), tck=blk, cache=vp_all)
        qi_hm, qi_hm_s = _proj_call(hp, hs, w_in_t, l, cb_qi, idx_q, ropei_p, ropei_s, tm, half=D_IDX // 2,
                                    outs=("idx_heads",), outs_s=("idx_heads",))
        kw, ki16, wi_t, kw_s, ki16_s = _kidx_call(hp, hs, w_in_t, l, o_ki // LANES, ropei_p, ropei_s, tm)

        att = _dsa_call(q, qi_hm, wi_t, k16, v_t, ki16, topk_p, blk)
        qi_hb = qi_hm_s[:, :db].reshape(N_IDX_HEADS * db, D_IDX)
        w_hb = kw_s[:db, D_IDX:D_IDX + N_IDX_HEADS].T.reshape(N_IDX_HEADS * db, 1)
        scores = _pool_scores_call(qi_hb, w_hb, kidx_t, l, db, _pages_per_step(n_pool))
        rows = _select_call(page_table, scores.reshape(db, n_pool, PAGE_SIZE), qi_hb, w_hb, ki16_s[:db], topk_s)
        att_s = _gather_attend_call(rows.reshape(db, topk_s),
                                    q_s[:db].reshape(db, N_HEADS, HEAD_DIM),
                                    k_s[:db].reshape(db, N_HEADS, HEAD_DIM),
                                    v_s[:db].reshape(db, N_HEADS, HEAD_DIM),
                                    ck_rows, cv_rows, l, topk_s)
        att_s = jnp.zeros((rs, att_dim), BF16).at[:db].set(att_s.reshape(db, att_dim).astype(BF16))

        merged, merged_s = _merge_call(hp, a_in, att, hs, a_in_s, att_s, w_in_t, o_ga, w_pa, w_pb, l, tm_full, tn)
        xp, h2, xs, h2_s = _resid_norm_call(xp, merged, xs, merged_s, w_o16, l, g_ffn[l],
                                            gt1, sc2, sh2, gt1s, sc2s, sh2s, tm_full)
        act, act_s = _ffn_up_call(h2, h2_s, w_gate, w_up, l, tm, tn)
        xp, nxt, xs, nxt_s = _ffn_down_call(xp, act, xs, act_s, w_down16, l, g_next, gt2, mod_p[ln, :, :, 1],
                                            mod_p[ln, :, :, 0], gt2s, mod_s[ln, :, 1], mod_s[ln, :, 0],
                                            tm_full, tk_ffn, last)
        if last:
            yp, ys = nxt, nxt_s
        else:
            hp, hs = nxt, nxt_s
        outs["kip"].append(kw[:, :, :D_IDX].reshape(bp, tp // PAGE_SIZE, PAGE_SIZE, D_IDX))
        outs["cp"].append(conv_st)
        outs["ks"].append(k_s[:db].reshape(db, ts, N_HEADS, HEAD_DIM))
        outs["vs"].append(v_s[:db].reshape(db, ts, N_HEADS, HEAD_DIM))
        outs["kis"].append(kw_s[:db, :D_IDX].reshape(db, ts, D_IDX))
        outs["cs"].append(jnp.stack([prev_s[:db, 1], u_s[:db]], axis=1))

    page_shape = (depth, bp, tp // PAGE_SIZE, PAGE_SIZE, N_HEADS, HEAD_DIM)
    return (yp, ys[:db].reshape(db, ts, d),
            kp_all.reshape(page_shape), vp_all.reshape(page_shape), jnp.stack(outs["kip"]), jnp.stack(outs["cp"]),
            jnp.stack(outs["ks"]), jnp.stack(outs["vs"]), jnp.stack(outs["kis"]), jnp.stack(outs["cs"]))
```
